```python
import jax, jax.numpy as jnp
from jax import lax
import numpy as np


D_MODEL = 1024
BATCH = 8
SEQ = 2048
DEPTH = 4
DEC_BATCH = 128
DEC_SEQ = 8
PAST_LEN = 8192
PAGE_SIZE = 128

POOL_WINDOWS = (2, 4, 8, 16)
POOL_GROUP = 64
POOL_WIDTH = POOL_GROUP * len(POOL_WINDOWS)
POOL_BUF = max(POOL_WINDOWS) - 1

N_HEADS = 8
QK_NOPE = 64
QK_ROPE = 32
V_DIM = 64
Q_LORA = 256
KV_LORA = 128
ROPE_THETA = 10000.0
Q_BLOCK = 128
ATTN_WIDTH = N_HEADS * V_DIM
ATTN_SCALE = (QK_NOPE + QK_ROPE) ** -0.5

CONV_WIDTH = 256
CONV_K = 31
CONV_BUF = CONV_K - 1

N_BRANCH = 3
MIX_WIDTH = POOL_WIDTH + ATTN_WIDTH + CONV_WIDTH

OFF_POOL = 0
OFF_Q = OFF_POOL + POOL_WIDTH
OFF_KV = OFF_Q + Q_LORA
OFF_KR = OFF_KV + KV_LORA
OFF_GLU = OFF_KR + QK_ROPE
OFF_GATE = OFF_GLU + 2 * CONV_WIDTH
IN_COLS = OFF_GATE + N_BRANCH * D_MODEL

D_FF = 2816
N_EXPERTS = 8
TOP_K = 2
D_FF_EXPERT = 3584
N_DENSE = (DEPTH + 1) // 2
N_MOE = DEPTH // 2

EPS = 1e-6
NEG_INF = -1e30

kernel_name = "hybrid_pool_mla_conformer_decoder_step"


def rmsnorm(x, g):
    xf = x.astype(jnp.float32)
    y = xf * lax.rsqrt(jnp.mean(xf * xf, axis=-1, keepdims=True) + EPS)
    return (y * g.astype(jnp.float32)).astype(x.dtype)


def layernorm(x, g, b):
    xf = x.astype(jnp.float32)
    mu = jnp.mean(xf, axis=-1, keepdims=True)
    xc = xf - mu
    y = xc * lax.rsqrt(jnp.mean(xc * xc, axis=-1, keepdims=True) + EPS)
    return (y * g.astype(jnp.float32) + b.astype(jnp.float32)).astype(x.dtype)


def rope_cos_sin(pos):
    inv = ROPE_THETA ** (-jnp.arange(0, QK_ROPE, 2, dtype=jnp.float32) / QK_ROPE)
    ang = pos.astype(jnp.float32)[:, None] * inv[None, :]
    return jnp.cos(ang), jnp.sin(ang)


def apply_rope(x, cos, sin):
    half = QK_ROPE // 2
    xf = x.astype(jnp.float32)
    x1, x2 = xf[..., :half], xf[..., half:]
    return jnp.concatenate([x1 * cos - x2 * sin, x2 * cos + x1 * sin], axis=-1).astype(x.dtype)


def pool_mix(u, prev, prev_valid, w_pool, s_pool):
    T = u.shape[1]
    P = prev.shape[1]
    ext = jnp.concatenate([prev, u], axis=1).astype(jnp.float32)
    valid = jnp.concatenate([prev_valid, jnp.ones((T,), jnp.float32)])
    cs = jnp.pad(jnp.cumsum(ext * valid[None, :, None], axis=1), ((0, 0), (1, 0), (0, 0)))
    cn = jnp.pad(jnp.cumsum(valid), (1, 0))
    outs = []
    for g, w in enumerate(POOL_WINDOWS):
        lo, hi = g * POOL_GROUP, (g + 1) * POOL_GROUP
        ssum = cs[:, P + 1:P + 1 + T, lo:hi] - cs[:, P + 1 - w:P + 1 - w + T, lo:hi]
        cnt = cn[P + 1:P + 1 + T] - cn[P + 1 - w:P + 1 - w + T]
        pooled = ssum / cnt[None, :, None] - ext[:, P:, lo:hi]
        outs.append(jnp.einsum('btc,cd->btd', pooled, w_pool[g].astype(jnp.float32)))
    out = jnp.concatenate(outs, axis=-1) * s_pool.astype(jnp.float32)
    return out.astype(u.dtype)


def mla_scores_out(q_abs, q_rope, ckv, krope, q_pos, k_pos):
    s = (jnp.einsum('bqhr,bkr->bhqk', q_abs, ckv)
         + jnp.einsum('bqhd,bkd->bhqk', q_rope, krope)).astype(jnp.float32) * ATTN_SCALE
    mask = k_pos[None, :] <= q_pos[:, None]
    s = jnp.where(mask[None, None], s, NEG_INF)
    p = jax.nn.softmax(s, axis=-1).astype(ckv.dtype)
    return jnp.einsum('bhqk,bkr->bqhr', p, ckv)


def mla_prompt(q_abs, q_rope, ckv, krope):
    B, T = q_abs.shape[0], q_abs.shape[1]
    nb = T // Q_BLOCK
    qa = q_abs.reshape(B, nb, Q_BLOCK, N_HEADS, KV_LORA).transpose(1, 0, 2, 3, 4)
    qr = q_rope.reshape(B, nb, Q_BLOCK, N_HEADS, QK_ROPE).transpose(1, 0, 2, 3, 4)
    k_pos = jnp.arange(T)
    starts = jnp.arange(nb) * Q_BLOCK

    def block(args):
        qa_b, qr_b, start = args
        q_pos = start + jnp.arange(Q_BLOCK)
        return mla_scores_out(qa_b, qr_b, ckv, krope, q_pos, k_pos)

    out = lax.map(block, (qa, qr, starts))
    return out.transpose(1, 0, 2, 3, 4).reshape(B, T, N_HEADS, KV_LORA)


def mla_sample(q_abs, q_rope, ckv_new, krope_new, past_ckv, past_krope, pos):
    ckv = jnp.concatenate([past_ckv, ckv_new], axis=1)
    krope = jnp.concatenate([past_krope, krope_new], axis=1)
    k_pos = jnp.arange(ckv.shape[1])
    return mla_scores_out(q_abs, q_rope, ckv, krope, pos, k_pos)


def mixer_block(h, pos, pool_prev, pool_valid, conv_prev, past_ckv, past_krope,
                w_in, b_gate, w_pool, s_pool, g_q_lat, w_uq, g_kv_lat, w_uk, w_uv,
                w_dw, b_dw, g_conv_ln, b_conv_ln, w_br, w_out):
    B, T = h.shape[0], h.shape[1]
    proj = jnp.einsum('btd,dc->btc', h, w_in)
    u_pool = proj[..., OFF_POOL:OFF_Q]
    c_q = rmsnorm(proj[..., OFF_Q:OFF_KV], g_q_lat)
    c_kv = rmsnorm(proj[..., OFF_KV:OFF_KR], g_kv_lat)
    cos, sin = rope_cos_sin(pos)
    k_r = apply_rope(proj[..., OFF_KR:OFF_GLU], cos, sin)
    glu = proj[..., OFF_GLU:OFF_GATE]
    gates = jax.nn.sigmoid((proj[..., OFF_GATE:] + b_gate).astype(jnp.float32)).astype(h.dtype)

    o_pool = pool_mix(u_pool, pool_prev, pool_valid, w_pool, s_pool)
    new_pool = jnp.concatenate([pool_prev, u_pool], axis=1)[:, -POOL_BUF:]

    q = jnp.einsum('btr,rhd->bthd', c_q, w_uq)
    q_nope = q[..., :QK_NOPE]
    q_rope = apply_rope(q[..., QK_NOPE:], cos[:, None, :], sin[:, None, :])
    q_abs = jnp.einsum('bthd,rhd->bthr', q_nope, w_uk)
    if past_ckv is None:
        lat = mla_prompt(q_abs, q_rope, c_kv, k_r)
    else:
        lat = mla_sample(q_abs, q_rope, c_kv, k_r, past_ckv, past_krope, pos)
    o_attn = jnp.einsum('bthr,rhd->bthd', lat, w_uv).reshape(B, T, ATTN_WIDTH)

    z = glu[..., :CONV_WIDTH] * jax.nn.sigmoid(glu[..., CONV_WIDTH:])
    z_ext = jnp.concatenate([conv_prev, z], axis=1)
    zc = lax.conv_general_dilated(z_ext, w_dw[:, None, :], (1,), 'VALID',
                                  dimension_numbers=('NWC', 'WIO', 'NWC'),
                                  feature_group_count=CONV_WIDTH) + b_dw
    o_conv = jax.nn.silu(layernorm(zc, g_conv_ln, b_conv_ln))
    new_conv = z_ext[:, -CONV_BUF:]

    br_a = jnp.einsum('btc,cd->btd', o_pool, w_br[:POOL_WIDTH])
    br_b = jnp.einsum('btc,cd->btd', o_attn, w_br[POOL_WIDTH:POOL_WIDTH + ATTN_WIDTH])
    br_c = jnp.einsum('btc,cd->btd', o_conv, w_br[POOL_WIDTH + ATTN_WIDTH:])
    merged = (gates[..., :D_MODEL] * br_a + gates[..., D_MODEL:2 * D_MODEL] * br_b
              + gates[..., 2 * D_MODEL:] * br_c)
    y = jnp.einsum('btd,de->bte', merged, w_out)
    return y, c_kv, k_r, new_pool, new_conv


def swiglu(h, wg, wu, wd):
    a = jnp.einsum('btd,df->btf', h, wg)
    b = jnp.einsum('btd,df->btf', h, wu)
    return jnp.einsum('btf,fd->btd', jax.nn.silu(a) * b, wd)


def moe(h, w_router, wg, wu, wd):
    logits = jnp.einsum('btd,de->bte', h, w_router).astype(jnp.float32)
    top_l, top_i = lax.top_k(logits, TOP_K)
    top_w = jax.nn.softmax(top_l, axis=-1)
    comb = jnp.sum(jax.nn.one_hot(top_i, N_EXPERTS, dtype=jnp.float32) * top_w[..., None],
                   axis=-2).astype(h.dtype)
    out = jnp.zeros_like(h)
    for e in range(N_EXPERTS):
        out = out + comb[..., e:e + 1] * swiglu(h, wg[e], wu[e], wd[e])
    return out


def trunk(x, pos, state_pool, state_conv, cache_ckv, cache_krope, page_table, params):
    (g_mix_norm, w_in, b_gate, w_pool, s_pool, g_q_lat, w_uq, g_kv_lat, w_uk, w_uv,
     w_dw, b_dw, g_conv_ln, b_conv_ln, w_br, w_out, g_ffn_norm,
     w_d_gate, w_d_up, w_d_down, w_router, w_e_gate, w_e_up, w_e_down, g_final) = params
    B = x.shape[0]
    ckvs, krs, pools, convs = [], [], [], []
    for l in range(DEPTH):
        h = rmsnorm(x, g_mix_norm[l])
        if cache_ckv is None:
            pool_prev = jnp.zeros((B, POOL_BUF, POOL_WIDTH), x.dtype)
            pool_valid = jnp.zeros((POOL_BUF,), jnp.float32)
            conv_prev = jnp.zeros((B, CONV_BUF, CONV_WIDTH), x.dtype)
            past_ckv, past_krope = None, None
        else:
            pool_prev = state_pool[l]
            pool_valid = jnp.ones((POOL_BUF,), jnp.float32)
            conv_prev = state_conv[l]
            past_ckv = cache_ckv[l, page_table].reshape(B, -1, KV_LORA)
            past_krope = cache_krope[l, page_table].reshape(B, -1, QK_ROPE)
        y, ckv, kr, npool, nconv = mixer_block(
            h, pos, pool_prev, pool_valid, conv_prev, past_ckv, past_krope,
            w_in[l], b_gate[l], w_pool[l], s_pool[l], g_q_lat[l], w_uq[l], g_kv_lat[l],
            w_uk[l], w_uv[l], w_dw[l], b_dw[l], g_conv_ln[l], b_conv_ln[l], w_br[l], w_out[l])
        x = x + y
        h = rmsnorm(x, g_ffn_norm[l])
        if l % 2 == 0:
            f = swiglu(h, w_d_gate[l // 2], w_d_up[l // 2], w_d_down[l // 2])
        else:
            f = moe(h, w_router[l // 2], w_e_gate[l // 2], w_e_up[l // 2], w_e_down[l // 2])
        x = x + f
        ckvs.append(ckv)
        krs.append(kr)
        pools.append(npool)
        convs.append(nconv)
    return (rmsnorm(x, g_final), jnp.stack(ckvs), jnp.stack(krs), jnp.stack(pools), jnp.stack(convs))


def setup_inputs(seed: int = 0) -> dict:
    key = jax.random.key(seed)
    ks = iter(jax.random.split(key, 48))

    def nrm(shape, scale):
        return jax.random.normal(next(ks), shape, jnp.float32) * scale

    def gain(shape):
        return 1.0 + nrm(shape, 0.1)

    n_pages = PAST_LEN // PAGE_SIZE
    n_used = DEC_BATCH * n_pages
    n_pool = n_used + max(1, n_used // 4)
    d = {}
    d['x_prompt'] = nrm((BATCH, SEQ, D_MODEL), 1.0)
    d['x_sample'] = nrm((DEC_BATCH, DEC_SEQ, D_MODEL), 1.0)
    d['cache_ckv'] = nrm((DEPTH, n_pool, PAGE_SIZE, KV_LORA), 1.0)
    d['cache_krope'] = nrm((DEPTH, n_pool, PAGE_SIZE, QK_ROPE), 1.0)
    d['page_table'] = jax.random.permutation(next(ks), n_pool)[:n_used].reshape(DEC_BATCH, n_pages).astype(jnp.int32)
    d['state_pool'] = nrm((DEPTH, DEC_BATCH, POOL_BUF, POOL_WIDTH), 1.0)
    d['state_conv'] = nrm((DEPTH, DEC_BATCH, CONV_BUF, CONV_WIDTH), 0.5)
    d['g_mix_norm'] = gain((DEPTH, D_MODEL))
    d['w_in'] = nrm((DEPTH, D_MODEL, IN_COLS), D_MODEL ** -0.5)
    d['b_gate'] = nrm((DEPTH, N_BRANCH * D_MODEL), 0.1)
    d['w_pool'] = nrm((DEPTH, len(POOL_WINDOWS), POOL_GROUP, POOL_GROUP), POOL_GROUP ** -0.5)
    d['s_pool'] = gain((DEPTH, POOL_WIDTH))
    d['g_q_lat'] = gain((DEPTH, Q_LORA))
    d['w_uq'] = nrm((DEPTH, Q_LORA, N_HEADS, QK_NOPE + QK_ROPE), Q_LORA ** -0.5)
    d['g_kv_lat'] = gain((DEPTH, KV_LORA))
    d['w_uk'] = nrm((DEPTH, KV_LORA, N_HEADS, QK_NOPE), KV_LORA ** -0.5)
    d['w_uv'] = nrm((DEPTH, KV_LORA, N_HEADS, V_DIM), KV_LORA ** -0.5)
    d['w_dw'] = nrm((DEPTH, CONV_K, CONV_WIDTH), CONV_K ** -0.5)
    d['b_dw'] = nrm((DEPTH, CONV_WIDTH), 0.02)
    d['g_conv_ln'] = gain((DEPTH, CONV_WIDTH))
    d['b_conv_ln'] = nrm((DEPTH, CONV_WIDTH), 0.02)
    d['w_br'] = nrm((DEPTH, MIX_WIDTH, D_MODEL), (MIX_WIDTH // N_BRANCH) ** -0.5)
    d['w_out'] = nrm((DEPTH, D_MODEL, D_MODEL), D_MODEL ** -0.5)
    d['g_ffn_norm'] = gain((DEPTH, D_MODEL))
    d['w_d_gate'] = nrm((N_DENSE, D_MODEL, D_FF), D_MODEL ** -0.5)
    d['w_d_up'] = nrm((N_DENSE, D_MODEL, D_FF), D_MODEL ** -0.5)
    d['w_d_down'] = nrm((N_DENSE, D_FF, D_MODEL), D_FF ** -0.5)
    d['w_router'] = nrm((N_MOE, D_MODEL, N_EXPERTS), D_MODEL ** -0.5)
    d['w_e_gate'] = nrm((N_MOE, N_EXPERTS, D_MODEL, D_FF_EXPERT), D_MODEL ** -0.5)
    d['w_e_up'] = nrm((N_MOE, N_EXPERTS, D_MODEL, D_FF_EXPERT), D_MODEL ** -0.5)
    d['w_e_down'] = nrm((N_MOE, N_EXPERTS, D_FF_EXPERT, D_MODEL), D_FF_EXPERT ** -0.5)
    d['g_final'] = gain((D_MODEL,))
    return d


def reference(x_prompt, x_sample, cache_ckv, cache_krope, page_table, state_pool, state_conv,
              g_mix_norm, w_in, b_gate, w_pool, s_pool, g_q_lat, w_uq, g_kv_lat, w_uk, w_uv,
              w_dw, b_dw, g_conv_ln, b_conv_ln, w_br, w_out, g_ffn_norm,
              w_d_gate, w_d_up, w_d_down, w_router, w_e_gate, w_e_up, w_e_down, g_final):
    params = (g_mix_norm, w_in, b_gate, w_pool, s_pool, g_q_lat, w_uq, g_kv_lat, w_uk, w_uv,
              w_dw, b_dw, g_conv_ln, b_conv_ln, w_br, w_out, g_ffn_norm,
              w_d_gate, w_d_up, w_d_down, w_router, w_e_gate, w_e_up, w_e_down, g_final)
    pos_prompt = jnp.arange(x_prompt.shape[1])
    past_len = page_table.shape[1] * PAGE_SIZE
    pos_sample = past_len + jnp.arange(x_sample.shape[1])
    y_prompt, new_ckv_prompt, new_krope_prompt, new_pool_prompt, new_conv_prompt = trunk(
        x_prompt, pos_prompt, None, None, None, None, None, params)
    y_sample, new_ckv_sample, new_krope_sample, new_pool_sample, new_conv_sample = trunk(
        x_sample, pos_sample, state_pool, state_conv, cache_ckv, cache_krope, page_table, params)
    return (y_prompt, y_sample, new_ckv_prompt, new_krope_prompt, new_pool_prompt, new_conv_prompt,
            new_ckv_sample, new_krope_sample, new_pool_sample, new_conv_sample)
```

```python
import functools

import jax
import jax.numpy as jnp
from jax import lax
from jax.experimental import pallas as pl
from jax.experimental.pallas import tpu as pltpu

F32 = jnp.float32
BF16 = jnp.bfloat16

D_MODEL = 1024
BATCH = 8
SEQ = 2048
DEPTH = 4
DEC_BATCH = 128
DEC_SEQ = 8
PAST_LEN = 8192
PAGE_SIZE = 128
N_PAGES = PAST_LEN // PAGE_SIZE

POOL_WINDOWS = (2, 4, 8, 16)
POOL_GROUP = 64
POOL_WIDTH = 256
POOL_BUF = 15

N_HEADS = 8
QK_NOPE = 64
QK_ROPE = 32
V_DIM = 64
Q_LORA = 256
KV_LORA = 128
ROPE_THETA = 10000.0
ATTN_WIDTH = N_HEADS * V_DIM
ATTN_SCALE = (QK_NOPE + QK_ROPE) ** -0.5

CONV_WIDTH = 256
CONV_K = 31
CONV_BUF = CONV_K - 1

N_BRANCH = 3
OFF_POOL = 0
OFF_Q = OFF_POOL + POOL_WIDTH
OFF_KV = OFF_Q + Q_LORA
OFF_KR = OFF_KV + KV_LORA
OFF_GLU = OFF_KR + QK_ROPE
OFF_GATE = OFF_GLU + 2 * CONV_WIDTH

D_FF = 2816
N_EXPERTS = 8
D_FF_EXPERT = 3584

EPS = 1e-6
NEG_INF = -1e30

N_PROMPT = BATCH * SEQ
N_SAMPLE = DEC_BATCH * DEC_SEQ
N_TOK = N_PROMPT + N_SAMPLE

LANES = 128
VMEM_BYTES_V7X = 64 * 1024 * 1024

C_POOL = 0
C_Q = C_POOL + POOL_WIDTH
C_KV = C_Q + Q_LORA
C_KRA = C_KV + KV_LORA
C_KRB = C_KRA + LANES
C_GLU = C_KRB + LANES
C_GATE = C_GLU + 2 * CONV_WIDTH
C_END = C_GATE + N_BRANCH * D_MODEL

Q_HEAD = 2 * LANES
Q_COLS = N_HEADS * Q_HEAD
LAT_COLS = N_HEADS * KV_LORA

TM_TOKEN = 512
TM_DENSE = 512
TF_DENSE = 1408
TM_EXPERT = 1024
TF_EXPERT = 512
TQ = 256
POOL_HALO = 16
CONV_HALO = 32
SEQ_CHUNK = 256
CONV_ROWS = 64
SAMPLE_BB = 16
KEYS_SAMPLE = PAST_LEN + LANES


def _cparams(semantics, vmem_mib):
    assert vmem_mib * 1024 * 1024 < VMEM_BYTES_V7X
    return pltpu.CompilerParams(dimension_semantics=semantics,
                                vmem_limit_bytes=vmem_mib * 1024 * 1024)


def _rms(x, g):
    return x * lax.rsqrt(jnp.mean(x * x, axis=-1, keepdims=True) + EPS) * g


def _dot(a, b):
    return jnp.dot(a, b, preferred_element_type=F32)


def _inproj_kernel(x_ref, g_ref, w_ref, bg_ref, gq_ref, gkv_ref, cos_ref, sin_ref,
                   up_ref, cq_ref, kvb_ref, ckv_ref, kr_ref, z_ref, gate_ref):
    h = _rms(x_ref[...], g_ref[...]).astype(BF16)

    def proj(lo, hi):
        return _dot(h, w_ref[:, lo:hi])

    up_ref[...] = proj(C_POOL, C_Q)
    cq_ref[...] = _rms(proj(C_Q, C_KV), gq_ref[...]).astype(BF16)
    ckv = _rms(proj(C_KV, C_KRA), gkv_ref[...])
    kr = proj(C_KRA, C_KRB) * cos_ref[...] + proj(C_KRB, C_GLU) * sin_ref[...]
    ckv_ref[...] = ckv
    kr_ref[...] = kr[:, :QK_ROPE]
    kvb_ref[:, :KV_LORA] = ckv.astype(BF16)
    kvb_ref[:, KV_LORA:] = kr.astype(BF16)
    glu = proj(C_GLU, C_GATE)
    z_ref[...] = glu[:, :CONV_WIDTH] * jax.nn.sigmoid(glu[:, CONV_WIDTH:])
    for c in range(N_BRANCH):
        lo = c * D_MODEL
        g = proj(C_GATE + lo, C_GATE + lo + D_MODEL) + bg_ref[:, lo:lo + D_MODEL]
        gate_ref[:, lo:lo + D_MODEL] = jax.nn.sigmoid(g).astype(BF16)


def _inproj(x, g, w, bg, gq, gkv, cos, sin):
    tm = TM_TOKEN
    row = lambda i: (i, 0)
    fix = lambda i: (0, 0)
    return pl.pallas_call(
        _inproj_kernel,
        grid=(N_TOK // tm,),
        in_specs=[
            pl.BlockSpec((tm, D_MODEL), row),
            pl.BlockSpec((1, D_MODEL), fix),
            pl.BlockSpec((D_MODEL, C_END), fix),
            pl.BlockSpec((1, N_BRANCH * D_MODEL), fix),
            pl.BlockSpec((1, Q_LORA), fix),
            pl.BlockSpec((1, KV_LORA), fix),
            pl.BlockSpec((tm, LANES), row),
            pl.BlockSpec((tm, LANES), row),
        ],
        out_specs=[
            pl.BlockSpec((tm, POOL_WIDTH), row),
            pl.BlockSpec((tm, Q_LORA), row),
            pl.BlockSpec((tm, Q_HEAD), row),
            pl.BlockSpec((tm, KV_LORA), row),
            pl.BlockSpec((tm, QK_ROPE), row),
            pl.BlockSpec((tm, CONV_WIDTH), row),
            pl.BlockSpec((tm, N_BRANCH * D_MODEL), row),
        ],
        out_shape=[
            jax.ShapeDtypeStruct((N_TOK, POOL_WIDTH), F32),
            jax.ShapeDtypeStruct((N_TOK, Q_LORA), BF16),
            jax.ShapeDtypeStruct((N_TOK, Q_HEAD), BF16),
            jax.ShapeDtypeStruct((N_TOK, KV_LORA), F32),
            jax.ShapeDtypeStruct((N_TOK, QK_ROPE), F32),
            jax.ShapeDtypeStruct((N_TOK, CONV_WIDTH), F32),
            jax.ShapeDtypeStruct((N_TOK, N_BRANCH * D_MODEL), BF16),
        ],
        compiler_params=_cparams(("parallel",), 56),
        name="inproj",
    )(x, g, w, bg, gq, gkv, cos, sin)


def _qproj_kernel(cq_ref, wn_ref, wa_ref, wb_ref, wuk_ref, cos_ref, sin_ref, q_ref):
    cq = cq_ref[...]
    qn = _dot(cq, wn_ref[...]).astype(BF16)
    ra = _dot(cq, wa_ref[...])
    rb = _dot(cq, wb_ref[...])
    cos = cos_ref[...]
    sin = sin_ref[...]
    for h in range(N_HEADS):
        sl = slice(h * LANES, (h + 1) * LANES)
        q_ref[:, h * Q_HEAD:h * Q_HEAD + LANES] = _dot(qn[:, sl], wuk_ref[h]).astype(BF16)
        q_ref[:, h * Q_HEAD + LANES:(h + 1) * Q_HEAD] = (
            ra[:, sl] * cos + rb[:, sl] * sin).astype(BF16)


def _qproj(cq, wn, wa, wb, wuk, cos, sin):
    tm = TM_TOKEN
    row = lambda i: (i, 0)
    fix = lambda i: (0, 0)
    return pl.pallas_call(
        _qproj_kernel,
        grid=(N_TOK // tm,),
        in_specs=[
            pl.BlockSpec((tm, Q_LORA), row),
            pl.BlockSpec((Q_LORA, N_HEADS * LANES), fix),
            pl.BlockSpec((Q_LORA, N_HEADS * LANES), fix),
            pl.BlockSpec((Q_LORA, N_HEADS * LANES), fix),
            pl.BlockSpec((N_HEADS, LANES, KV_LORA), lambda i: (0, 0, 0)),
            pl.BlockSpec((tm, LANES), row),
            pl.BlockSpec((tm, LANES), row),
        ],
        out_specs=pl.BlockSpec((tm, Q_COLS), row),
        out_shape=jax.ShapeDtypeStruct((N_TOK, Q_COLS), BF16),
        compiler_params=_cparams(("parallel",), 40),
        name="qproj",
    )(cq, wn, wa, wb, wuk, cos, sin)


def _pool_compute(ext_ref, n_rows, cnt_of_window, wbd_ref, sp_ref):
    def ld(j):
        return ext_ref[:, pl.ds(POOL_HALO - j, n_rows), :]

    tok = ld(0)
    run = tok
    sums = {}
    for j in range(1, max(POOL_WINDOWS)):
        run = run + ld(j)
        if j + 1 in POOL_WINDOWS:
            sums[j + 1] = run
    lane = lax.broadcasted_iota(jnp.int32, tok.shape, 2)
    pooled = sums[POOL_WINDOWS[-1]] / cnt_of_window(POOL_WINDOWS[-1])
    for g in range(len(POOL_WINDOWS) - 2, -1, -1):
        w = POOL_WINDOWS[g]
        pooled = jnp.where(lane < (g + 1) * POOL_GROUP, sums[w] / cnt_of_window(w), pooled)
    pooled = (pooled - tok).reshape(-1, POOL_WIDTH).astype(BF16)
    return _dot(pooled, wbd_ref[...]) * sp_ref[...]


def _pool_prompt_kernel(halo_ref, u_ref, wbd_ref, sp_ref, o_ref, ext_ref):
    i = pl.program_id(1)
    ext_ref[:, :POOL_HALO, :] = jnp.where(i > 0, halo_ref[...], 0.0)
    ext_ref[:, POOL_HALO:, :] = u_ref[...]
    pos = lax.broadcasted_iota(jnp.int32, (1, SEQ_CHUNK, 1), 1) + i * SEQ_CHUNK

    def cnt(w):
        return jnp.minimum(pos + 1, w).astype(F32)

    o_ref[0] = _pool_compute(ext_ref, SEQ_CHUNK, cnt, wbd_ref, sp_ref).astype(BF16)


def _pool_prompt(u, wbd, sp):
    hb = SEQ_CHUNK // POOL_HALO
    return pl.pallas_call(
        _pool_prompt_kernel,
        grid=(BATCH, SEQ // SEQ_CHUNK),
        in_specs=[
            pl.BlockSpec((1, POOL_HALO, POOL_WIDTH),
                         lambda b, i: (b, jnp.maximum(i * hb - 1, 0), 0)),
            pl.BlockSpec((1, SEQ_CHUNK, POOL_WIDTH), lambda b, i: (b, i, 0)),
            pl.BlockSpec((POOL_WIDTH, POOL_WIDTH), lambda b, i: (0, 0)),
            pl.BlockSpec((1, POOL_WIDTH), lambda b, i: (0, 0)),
        ],
        out_specs=pl.BlockSpec((1, SEQ_CHUNK, POOL_WIDTH), lambda b, i: (b, i, 0)),
        out_shape=jax.ShapeDtypeStruct((BATCH, SEQ, POOL_WIDTH), BF16),
        scratch_shapes=[pltpu.VMEM((1, POOL_HALO + SEQ_CHUNK, POOL_WIDTH), F32)],
        compiler_params=_cparams(("parallel", "parallel"), 32),
        name="pool_prompt",
    )(u, u, wbd, sp)


def _pool_sample_kernel(ext_ref, wbd_ref, sp_ref, o_ref):
    out = _pool_compute(ext_ref, DEC_SEQ, lambda w: float(w), wbd_ref, sp_ref)
    o_ref[...] = out.reshape(SAMPLE_BB, DEC_SEQ, POOL_WIDTH).astype(BF16)


def _pool_sample(ext, wbd, sp):
    return pl.pallas_call(
        _pool_sample_kernel,
        grid=(DEC_BATCH // SAMPLE_BB,),
        in_specs=[
            pl.BlockSpec((SAMPLE_BB, POOL_HALO + DEC_SEQ, POOL_WIDTH), lambda i: (i, 0, 0)),
            pl.BlockSpec((POOL_WIDTH, POOL_WIDTH), lambda i: (0, 0)),
            pl.BlockSpec((1, POOL_WIDTH), lambda i: (0, 0)),
        ],
        out_specs=pl.BlockSpec((SAMPLE_BB, DEC_SEQ, POOL_WIDTH), lambda i: (i, 0, 0)),
        out_shape=jax.ShapeDtypeStruct((DEC_BATCH, DEC_SEQ, POOL_WIDTH), BF16),
        compiler_params=_cparams(("parallel",), 32),
        name="pool_sample",
    )(ext, wbd, sp)


def _conv_compute(ext_ref, row0, n_rows, wdw_ref, bdw_ref, g_ref, b_ref):
    lead = CONV_HALO - CONV_BUF
    acc = None
    for k in range(CONV_K):
        term = ext_ref[:, pl.ds(row0 + lead + k, n_rows), :] * wdw_ref[k:k + 1, :]
        acc = term if acc is None else acc + term
    zc = acc + bdw_ref[...]
    mu = jnp.mean(zc, axis=-1, keepdims=True)
    xc = zc - mu
    y = xc * lax.rsqrt(jnp.mean(xc * xc, axis=-1, keepdims=True) + EPS)
    y = y * g_ref[...] + b_ref[...]
    return y * jax.nn.sigmoid(y)


def _conv_prompt_kernel(halo_ref, z_ref, wdw_ref, bdw_ref, g_ref, b_ref, o_ref, ext_ref):
    i = pl.program_id(1)
    ext_ref[:, :CONV_HALO, :] = jnp.where(i > 0, halo_ref[...], 0.0)
    ext_ref[:, CONV_HALO:, :] = z_ref[...]
    for r0 in range(0, SEQ_CHUNK, CONV_ROWS):
        o_ref[:, r0:r0 + CONV_ROWS, :] = _conv_compute(
            ext_ref, r0, CONV_ROWS, wdw_ref, bdw_ref, g_ref, b_ref).astype(BF16)


def _conv_prompt(z, wdw, bdw, g, b):
    hb = SEQ_CHUNK // CONV_HALO
    vec = pl.BlockSpec((1, CONV_WIDTH), lambda bb, i: (0, 0))
    return pl.pallas_call(
        _conv_prompt_kernel,
        grid=(BATCH, SEQ // SEQ_CHUNK),
        in_specs=[
            pl.BlockSpec((1, CONV_HALO, CONV_WIDTH),
                         lambda bb, i: (bb, jnp.maximum(i * hb - 1, 0), 0)),
            pl.BlockSpec((1, SEQ_CHUNK, CONV_WIDTH), lambda bb, i: (bb, i, 0)),
            pl.BlockSpec((CONV_K, CONV_WIDTH), lambda bb, i: (0, 0)),
            vec, vec, vec,
        ],
        out_specs=pl.BlockSpec((1, SEQ_CHUNK, CONV_WIDTH), lambda bb, i: (bb, i, 0)),
        out_shape=jax.ShapeDtypeStruct((BATCH, SEQ, CONV_WIDTH), BF16),
        scratch_shapes=[pltpu.VMEM((1, CONV_HALO + SEQ_CHUNK, CONV_WIDTH), F32)],
        compiler_params=_cparams(("parallel", "parallel"), 32),
        name="conv_prompt",
    )(z, z, wdw, bdw, g, b)


def _conv_sample_kernel(ext_ref, wdw_ref, bdw_ref, g_ref, b_ref, o_ref):
    o_ref[...] = _conv_compute(ext_ref, 0, DEC_SEQ, wdw_ref, bdw_ref, g_ref, b_ref).astype(BF16)


def _conv_sample(ext, wdw, bdw, g, b):
    vec = pl.BlockSpec((1, CONV_WIDTH), lambda i: (0, 0))
    return pl.pallas_call(
        _conv_sample_kernel,
        grid=(DEC_BATCH // SAMPLE_BB,),
        in_specs=[
            pl.BlockSpec((SAMPLE_BB, CONV_HALO + DEC_SEQ, CONV_WIDTH), lambda i: (i, 0, 0)),
            pl.BlockSpec((CONV_K, CONV_WIDTH), lambda i: (0, 0)),
            vec, vec, vec,
        ],
        out_specs=pl.BlockSpec((SAMPLE_BB, DEC_SEQ, CONV_WIDTH), lambda i: (i, 0, 0)),
        out_shape=jax.ShapeDtypeStruct((DEC_BATCH, DEC_SEQ, CONV_WIDTH), BF16),
        compiler_params=_cparams(("parallel",), 32),
        name="conv_sample",
    )(ext, wdw, bdw, g, b)


def _attn_prompt_kernel(q_ref, kt_ref, v_ref, o_ref):
    qi = pl.program_id(1)
    row = lax.broadcasted_iota(jnp.int32, (TQ, TQ), 0)
    col = lax.broadcasted_iota(jnp.int32, (TQ, TQ), 1)
    for h in range(N_HEADS):
        q = q_ref[0, :, h * Q_HEAD:(h + 1) * Q_HEAD]

        def step(j, carry, q=q):
            m, l, acc = carry
            s = _dot(q, kt_ref[0, j]) * ATTN_SCALE
            s = jnp.where(col + j * TQ <= row + qi * TQ, s, NEG_INF)
            m_new = jnp.maximum(m, jnp.max(s, axis=-1, keepdims=True))
            alpha = jnp.exp(m - m_new)
            p = jnp.exp(s - m_new)
            l = alpha * l + jnp.sum(p, axis=-1, keepdims=True)
            acc = alpha * acc + _dot(p.astype(BF16), v_ref[0, j])
            return m_new, l, acc

        init = (jnp.full((TQ, 1), NEG_INF, F32), jnp.zeros((TQ, 1), F32),
                jnp.zeros((TQ, KV_LORA), F32))
        _, l, acc = lax.fori_loop(0, qi + 1, step, init)
        o_ref[0, :, h * KV_LORA:(h + 1) * KV_LORA] = (acc / l).astype(BF16)


def _attn_prompt(q, kt, v):
    nblk = SEQ // TQ
    return pl.pallas_call(
        _attn_prompt_kernel,
        grid=(BATCH, nblk),
        in_specs=[
            pl.BlockSpec((1, TQ, Q_COLS), lambda b, i: (b, i, 0)),
            pl.BlockSpec((1, nblk, Q_HEAD, TQ), lambda b, i: (b, 0, 0, 0)),
            pl.BlockSpec((1, nblk, TQ, KV_LORA), lambda b, i: (b, 0, 0, 0)),
        ],
        out_specs=pl.BlockSpec((1, TQ, LAT_COLS), lambda b, i: (b, i, 0)),
        out_shape=jax.ShapeDtypeStruct((BATCH, SEQ, LAT_COLS), BF16),
        compiler_params=_cparams(("parallel", "arbitrary"), 32),
        name="attn_prompt",
    )(q, kt, v)


def _attn_sample_kernel(pt_ref, q_ref, kvn_ref, *rest):
    ck_refs = rest[:N_PAGES]
    kr_refs = rest[N_PAGES:2 * N_PAGES]
    o_ref = rest[2 * N_PAGES]
    kall_ref = rest[2 * N_PAGES + 1]
    del pt_ref

    @pl.when(pl.program_id(0) == 0)
    def _():
        kall_ref[...] = jnp.zeros(kall_ref.shape, BF16)

    for p in range(N_PAGES):
        rows = slice(p * PAGE_SIZE, (p + 1) * PAGE_SIZE)
        kall_ref[rows, :KV_LORA] = ck_refs[p][...].astype(BF16)
        kall_ref[rows, KV_LORA:KV_LORA + QK_ROPE] = kr_refs[p][...].astype(BF16)
    kall_ref[PAST_LEN:PAST_LEN + 2 * DEC_SEQ, :] = kvn_ref[0]

    q = q_ref[0]
    s = lax.dot_general(q, kall_ref[...], (((1,), (1,)), ((), ())),
                        preferred_element_type=F32) * ATTN_SCALE
    t = lax.broadcasted_iota(jnp.int32, s.shape, 0) & (DEC_SEQ - 1)
    key = lax.broadcasted_iota(jnp.int32, s.shape, 1)
    s = jnp.where(key <= t + PAST_LEN, s, NEG_INF)
    m = jnp.max(s, axis=-1, keepdims=True)
    p = jnp.exp(s - m)
    l = jnp.sum(p, axis=-1, keepdims=True)
    out = _dot(p.astype(BF16), kall_ref[:, :KV_LORA])
    o_ref[0] = (out / l).astype(BF16)


def _attn_sample(layer, page_table, q, kv_new, cache_ckv, cache_krope):
    def page_spec(width, p):
        return pl.BlockSpec((None, None, PAGE_SIZE, width),
                            lambda b, pt: (layer, pt[b * N_PAGES + p], 0, 0))

    grid_spec = pltpu.PrefetchScalarGridSpec(
        num_scalar_prefetch=1,
        grid=(DEC_BATCH,),
        in_specs=(
            [pl.BlockSpec((1, N_HEADS * DEC_SEQ, Q_HEAD), lambda b, pt: (b, 0, 0)),
             pl.BlockSpec((1, 2 * DEC_SEQ, Q_HEAD), lambda b, pt: (b, 0, 0))]
            + [page_spec(KV_LORA, p) for p in range(N_PAGES)]
            + [page_spec(QK_ROPE, p) for p in range(N_PAGES)]),
        out_specs=pl.BlockSpec((1, N_HEADS * DEC_SEQ, KV_LORA), lambda b, pt: (b, 0, 0)),
        scratch_shapes=[pltpu.VMEM((KEYS_SAMPLE, Q_HEAD), BF16)],
    )
    return pl.pallas_call(
        _attn_sample_kernel,
        grid_spec=grid_spec,
        out_shape=jax.ShapeDtypeStruct((DEC_BATCH, N_HEADS * DEC_SEQ, KV_LORA), BF16),
        compiler_params=_cparams(("arbitrary",), 48),
        name="attn_sample",
    )(page_table, q, kv_new, *([cache_ckv] * N_PAGES), *([cache_krope] * N_PAGES))


def _merge_kernel(with_router, op_ref, lat_ref, oc_ref, gate_ref, x_ref, wuv_ref, wbr_ref,
                  wout_ref, gf_ref, wrh_ref, wrl_ref, xo_ref, h_ref, comb_ref):
    o_attn = _dot(lat_ref[...], wuv_ref[...]).astype(BF16)
    br_a = _dot(op_ref[...], wbr_ref[:POOL_WIDTH, :])
    br_b = _dot(o_attn, wbr_ref[POOL_WIDTH:POOL_WIDTH + ATTN_WIDTH, :])
    br_c = _dot(oc_ref[...], wbr_ref[POOL_WIDTH + ATTN_WIDTH:, :])
    merged = (gate_ref[:, :D_MODEL].astype(F32) * br_a
              + gate_ref[:, D_MODEL:2 * D_MODEL].astype(F32) * br_b
              + gate_ref[:, 2 * D_MODEL:].astype(F32) * br_c)
    xn = x_ref[...] + _dot(merged.astype(BF16), wout_ref[...])
    xo_ref[...] = xn
    hn = _rms(xn, gf_ref[...])
    h_ref[...] = hn.astype(BF16)
    if not with_router:
        comb_ref[...] = jnp.ones(comb_ref.shape, F32)
        return
    hh = hn.astype(BF16)
    hl = (hn - hh.astype(F32)).astype(BF16)
    lg = _dot(hh, wrh_ref[...]) + _dot(hl, wrh_ref[...]) + _dot(hh, wrl_ref[...])
    lane = lax.broadcasted_iota(jnp.int32, lg.shape, 1).astype(F32)
    lg = jnp.where(lane < N_EXPERTS, lg, NEG_INF)
    m1 = jnp.max(lg, axis=-1, keepdims=True)
    i1 = jnp.min(jnp.where(lg == m1, lane, float(LANES)), axis=-1, keepdims=True)
    lg2 = jnp.where(lane == i1, NEG_INF, lg)
    m2 = jnp.max(lg2, axis=-1, keepdims=True)
    i2 = jnp.min(jnp.where(lg2 == m2, lane, float(LANES)), axis=-1, keepdims=True)
    e = jnp.exp(m2 - m1)
    w1 = 1.0 / (1.0 + e)
    w2 = e / (1.0 + e)
    comb_ref[...] = jnp.where(lane == i1, w1, 0.0) + jnp.where(lane == i2, w2, 0.0)


def _merge(with_router, o_pool, lat, o_conv, gates, x, wuv, wbr, wout, gf, wrh, wrl):
    tm = TM_TOKEN
    row = lambda i: (i, 0)
    fix = lambda i: (0, 0)
    return pl.pallas_call(
        functools.partial(_merge_kernel, with_router),
        grid=(N_TOK // tm,),
        in_specs=[
            pl.BlockSpec((tm, POOL_WIDTH), row),
            pl.BlockSpec((tm, LAT_COLS), row),
            pl.BlockSpec((tm, CONV_WIDTH), row),
            pl.BlockSpec((tm, N_BRANCH * D_MODEL), row),
            pl.BlockSpec((tm, D_MODEL), row),
            pl.BlockSpec((LAT_COLS, ATTN_WIDTH), fix),
            pl.BlockSpec((D_MODEL, D_MODEL), fix),
            pl.BlockSpec((D_MODEL, D_MODEL), fix),
            pl.BlockSpec((1, D_MODEL), fix),
            pl.BlockSpec((D_MODEL, LANES), fix),
            pl.BlockSpec((D_MODEL, LANES), fix),
        ],
        out_specs=[
            pl.BlockSpec((tm, D_MODEL), row),
            pl.BlockSpec((tm, D_MODEL), row),
            pl.BlockSpec((tm, LANES), row),
        ],
        out_shape=[
            jax.ShapeDtypeStruct((N_TOK, D_MODEL), F32),
            jax.ShapeDtypeStruct((N_TOK, D_MODEL), BF16),
            jax.ShapeDtypeStruct((N_TOK, LANES), F32),
        ],
        compiler_params=_cparams(("parallel",), 48),
        name="merge_router" if with_router else "merge",
    )(o_pool, lat, o_conv, gates, x, wuv, wbr, wout, gf, wrh, wrl)


def _ffn_kernel(te_ref, h_ref, sc_ref, wg_ref, wu_ref, wd_ref, x_ref, o_ref, acc_ref):
    del te_ref
    j = pl.program_id(1)
    f = pl.program_id(2)

    @pl.when((j == 0) & (f == 0))
    def _():
        acc_ref[...] = jnp.zeros(acc_ref.shape, F32)

    h = h_ref[...]
    a = _dot(h, wg_ref[...])
    b = _dot(h, wu_ref[...])
    act = (a * jax.nn.sigmoid(a) * b).astype(BF16)
    lane = lax.broadcasted_iota(jnp.int32, sc_ref.shape, 1)
    scale = jnp.sum(jnp.where(lane == j, sc_ref[...], 0.0), axis=-1, keepdims=True)
    acc_ref[...] += scale * _dot(act, wd_ref[...])

    @pl.when((j == pl.num_programs(1) - 1) & (f == pl.num_programs(2) - 1))
    def _():
        o_ref[...] = x_ref[...] + acc_ref[...]


def _ffn(tile_expert, n_slots, h, scale, wg, wu, wd, x, tm, tf):
    n_tiles = N_TOK // tm
    d_ff = wg.shape[-1]
    row = lambda i, j, f, te: (i, 0)
    grid_spec = pltpu.PrefetchScalarGridSpec(
        num_scalar_prefetch=1,
        grid=(n_tiles, n_slots, d_ff // tf),
        in_specs=[
            pl.BlockSpec((tm, D_MODEL), row),
            pl.BlockSpec((tm, LANES), row),
            pl.BlockSpec((None, D_MODEL, tf), lambda i, j, f, te: (te[i * n_slots + j], 0, f)),
            pl.BlockSpec((None, D_MODEL, tf), lambda i, j, f, te: (te[i * n_slots + j], 0, f)),
            pl.BlockSpec((None, tf, D_MODEL), lambda i, j, f, te: (te[i * n_slots + j], f, 0)),
            pl.BlockSpec((tm, D_MODEL), row),
        ],
        out_specs=pl.BlockSpec((tm, D_MODEL), row),
        scratch_shapes=[pltpu.VMEM((tm, D_MODEL), F32)],
    )
    return pl.pallas_call(
        _ffn_kernel,
        grid_spec=grid_spec,
        out_shape=jax.ShapeDtypeStruct((N_TOK, D_MODEL), F32),
        compiler_params=_cparams(("parallel", "arbitrary", "arbitrary"), 56),
        name="ffn",
    )(tile_expert, h, scale, wg, wu, wd, x)


def _final_norm_kernel(x_ref, g_ref, o_ref):
    o_ref[...] = _rms(x_ref[...], g_ref[...])


def _final_norm(x, g):
    tm = TM_EXPERT
    return pl.pallas_call(
        _final_norm_kernel,
        grid=(N_TOK // tm,),
        in_specs=[pl.BlockSpec((tm, D_MODEL), lambda i: (i, 0)),
                  pl.BlockSpec((1, D_MODEL), lambda i: (0, 0))],
        out_specs=pl.BlockSpec((tm, D_MODEL), lambda i: (i, 0)),
        out_shape=jax.ShapeDtypeStruct((N_TOK, D_MODEL), F32),
        compiler_params=_cparams(("parallel",), 32),
        name="final_norm",
    )(x, g)


def _pad_cols(w, width):
    return jnp.pad(w, ((0, 0), (0, width - w.shape[1])))


def _swap_halves(w):
    half = QK_ROPE // 2
    return jnp.concatenate([w[..., half:], w[..., :half]], axis=-1)


def _prep_w_in(w):
    kr = w[:, OFF_KR:OFF_GLU]
    return jnp.concatenate([
        w[:, OFF_POOL:OFF_KR],
        _pad_cols(kr, LANES), _pad_cols(_swap_halves(kr), LANES),
        w[:, OFF_GLU:]], axis=1).astype(BF16)


def _prep_w_uq(w_uq):
    def per_head(part):
        pad = LANES - part.shape[-1]
        return jnp.pad(part, ((0, 0), (0, 0), (0, pad))).reshape(Q_LORA, N_HEADS * LANES).astype(BF16)

    rope = w_uq[..., QK_NOPE:]
    return per_head(w_uq[..., :QK_NOPE]), per_head(rope), per_head(_swap_halves(rope))


def _prep_w_uk(w_uk):
    wt = jnp.transpose(w_uk, (1, 2, 0))
    return jnp.pad(wt, ((0, 0), (0, LANES - QK_NOPE), (0, 0))).astype(BF16)


def _prep_w_uv(w_uv):
    eye = jnp.eye(N_HEADS, dtype=w_uv.dtype)
    wbd = jnp.einsum('rhd,hg->hrgd', w_uv, eye)
    return wbd.reshape(LAT_COLS, ATTN_WIDTH).astype(BF16)


def _prep_w_pool(w_pool):
    n = len(POOL_WINDOWS)
    eye = jnp.eye(n, dtype=w_pool.dtype)
    return jnp.einsum('gcd,gk->gckd', w_pool, eye).reshape(POOL_WIDTH, POOL_WIDTH).astype(BF16)


def _rope_tables():
    inv = ROPE_THETA ** (-jnp.arange(0, QK_ROPE, 2, dtype=F32) / QK_ROPE)
    pos = jnp.concatenate([jnp.tile(jnp.arange(SEQ), BATCH),
                           jnp.tile(PAST_LEN + jnp.arange(DEC_SEQ), DEC_BATCH)])
    ang = pos.astype(F32)[:, None] * inv[None, :]
    cos, sin = jnp.cos(ang), jnp.sin(ang)
    cos_t = _pad_cols(jnp.concatenate([cos, cos], axis=1), LANES)
    sin_t = _pad_cols(jnp.concatenate([-sin, sin], axis=1), LANES)
    return cos_t, sin_t


def _split_hi_lo(w):
    hi = w.astype(BF16)
    lo = (w - hi.astype(F32)).astype(BF16)
    return hi, lo


def kernel(x_prompt, x_sample, cache_ckv, cache_krope, page_table, state_pool, state_conv, g_mix_norm, w_in, b_gate, w_pool, s_pool, g_q_lat, w_uq, g_kv_lat, w_uk, w_uv, w_dw, b_dw, g_conv_ln, b_conv_ln, w_br, w_out, g_ffn_norm, w_d_gate, w_d_up, w_d_down, w_router, w_e_gate, w_e_up, w_e_down, g_final):
    x = jnp.concatenate([x_prompt.reshape(N_PROMPT, D_MODEL),
                         x_sample.reshape(N_SAMPLE, D_MODEL)], axis=0)
    cos_t, sin_t = _rope_tables()
    pt_flat = page_table.reshape(-1)
    ones_scale = jnp.ones((N_TOK, LANES), F32)

    ckv_p, kr_p, pool_p, conv_p = [], [], [], []
    ckv_s, kr_s, pool_s, conv_s = [], [], [], []
    for l in range(DEPTH):
        u_pool, c_q, kvb, c_kv, k_r, z, gates = _inproj(
            x, g_mix_norm[l][None], _prep_w_in(w_in[l]), b_gate[l][None],
            g_q_lat[l][None], g_kv_lat[l][None], cos_t, sin_t)

        wn, wa, wb = _prep_w_uq(w_uq[l])
        q = _qproj(c_q, wn, wa, wb, _prep_w_uk(w_uk[l]), cos_t, sin_t)

        wbd = _prep_w_pool(w_pool[l])
        sp = s_pool[l][None]
        u_p = u_pool[:N_PROMPT].reshape(BATCH, SEQ, POOL_WIDTH)
        u_s = u_pool[N_PROMPT:].reshape(DEC_BATCH, DEC_SEQ, POOL_WIDTH)
        pool_ext = jnp.concatenate(
            [jnp.zeros((DEC_BATCH, POOL_HALO - POOL_BUF, POOL_WIDTH), F32), state_pool[l], u_s], axis=1)
        o_pool = jnp.concatenate([
            _pool_prompt(u_p, wbd, sp).reshape(N_PROMPT, POOL_WIDTH),
            _pool_sample(pool_ext, wbd, sp).reshape(N_SAMPLE, POOL_WIDTH)], axis=0)

        z_p = z[:N_PROMPT].reshape(BATCH, SEQ, CONV_WIDTH)
        z_s = z[N_PROMPT:].reshape(DEC_BATCH, DEC_SEQ, CONV_WIDTH)
        conv_ext = jnp.concatenate(
            [jnp.zeros((DEC_BATCH, CONV_HALO - CONV_BUF, CONV_WIDTH), F32), state_conv[l], z_s], axis=1)
        conv_args = (w_dw[l], b_dw[l][None], g_conv_ln[l][None], b_conv_ln[l][None])
        o_conv = jnp.concatenate([
            _conv_prompt(z_p, *conv_args).reshape(N_PROMPT, CONV_WIDTH),
            _conv_sample(conv_ext, *conv_args).reshape(N_SAMPLE, CONV_WIDTH)], axis=0)

        nblk = SEQ // TQ
        kv_p = kvb[:N_PROMPT].reshape(BATCH, nblk, TQ, Q_HEAD)
        kt_p = jnp.swapaxes(kv_p, 2, 3)
        lat_p = _attn_prompt(q[:N_PROMPT].reshape(BATCH, SEQ, Q_COLS), kt_p, kv_p[..., :KV_LORA])
        q_s = q[N_PROMPT:].reshape(DEC_BATCH, DEC_SEQ, N_HEADS, Q_HEAD)
        q_s = jnp.swapaxes(q_s, 1, 2).reshape(DEC_BATCH, N_HEADS * DEC_SEQ, Q_HEAD)
        kv_new = jnp.pad(kvb[N_PROMPT:].reshape(DEC_BATCH, DEC_SEQ, Q_HEAD),
                         ((0, 0), (0, DEC_SEQ), (0, 0)))
        lat_s = _attn_sample(l, pt_flat, q_s, kv_new, cache_ckv, cache_krope)
        lat_s = jnp.swapaxes(lat_s.reshape(DEC_BATCH, N_HEADS, DEC_SEQ, KV_LORA), 1, 2)
        lat = jnp.concatenate([lat_p.reshape(N_PROMPT, LAT_COLS),
                               lat_s.reshape(N_SAMPLE, LAT_COLS)], axis=0)

        is_moe = l % 2 == 1
        wr = _pad_cols(w_router[l // 2], LANES) if is_moe else jnp.zeros((D_MODEL, LANES), F32)
        wrh, wrl = _split_hi_lo(wr)
        x, h, comb = _merge(is_moe, o_pool, lat, o_conv, gates, x, _prep_w_uv(w_uv[l]),
                            w_br[l].astype(BF16), w_out[l].astype(BF16), g_ffn_norm[l][None],
                            wrh, wrl)
        if is_moe:
            tile_expert = jnp.tile(jnp.arange(N_EXPERTS, dtype=jnp.int32), N_TOK // TM_EXPERT)
            x = _ffn(tile_expert, N_EXPERTS, h, comb, w_e_gate[l // 2].astype(BF16),
                     w_e_up[l // 2].astype(BF16), w_e_down[l // 2].astype(BF16), x,
                     TM_EXPERT, TF_EXPERT)
        else:
            tile_expert = jnp.zeros((N_TOK // TM_DENSE,), jnp.int32)
            x = _ffn(tile_expert, 1, h, ones_scale, w_d_gate[l // 2][None].astype(BF16),
                     w_d_up[l // 2][None].astype(BF16), w_d_down[l // 2][None].astype(BF16), x,
                     TM_DENSE, TF_DENSE)

        ckv_p.append(c_kv[:N_PROMPT].reshape(BATCH, SEQ, KV_LORA))
        kr_p.append(k_r[:N_PROMPT].reshape(BATCH, SEQ, QK_ROPE))
        pool_p.append(u_p[:, -POOL_BUF:])
        conv_p.append(z_p[:, -CONV_BUF:])
        ckv_s.append(c_kv[N_PROMPT:].reshape(DEC_BATCH, DEC_SEQ, KV_LORA))
        kr_s.append(k_r[N_PROMPT:].reshape(DEC_BATCH, DEC_SEQ, QK_ROPE))
        pool_s.append(jnp.concatenate([state_pool[l], u_s], axis=1)[:, -POOL_BUF:])
        conv_s.append(jnp.concatenate([state_conv[l], z_s], axis=1)[:, -CONV_BUF:])

    y = _final_norm(x, g_final[None])
    return (y[:N_PROMPT].reshape(BATCH, SEQ, D_MODEL),
            y[N_PROMPT:].reshape(DEC_BATCH, DEC_SEQ, D_MODEL),
            jnp.stack(ckv_p), jnp.stack(kr_p), jnp.stack(pool_p), jnp.stack(conv_p),
            jnp.stack(ckv_s), jnp.stack(kr_s), jnp.stack(pool_s), jnp.stack(conv_s))
```

```python
import functools

import jax
import jax.numpy as jnp
from jax import lax
from jax.experimental import pallas as pl
from jax.experimental.pallas import tpu as pltpu

F32 = jnp.float32
BF16 = jnp.bfloat16

D_MODEL = 1024
BATCH = 8
SEQ = 2048
DEPTH = 4
DEC_BATCH = 128
DEC_SEQ = 8
PAST_LEN = 8192
PAGE_SIZE = 128
N_PAGES = PAST_LEN // PAGE_SIZE

POOL_WINDOWS = (2, 4, 8, 16)
POOL_GROUP = 64
POOL_WIDTH = 256
POOL_BUF = 15

N_HEADS = 8
QK_NOPE = 64
QK_ROPE = 32
V_DIM = 64
Q_LORA = 256
KV_LORA = 128
ROPE_THETA = 10000.0
ATTN_WIDTH = N_HEADS * V_DIM
ATTN_SCALE = (QK_NOPE + QK_ROPE) ** -0.5

CONV_WIDTH = 256
CONV_K = 31
CONV_BUF = CONV_K - 1

N_BRANCH = 3
OFF_POOL = 0
OFF_Q = OFF_POOL + POOL_WIDTH
OFF_KV = OFF_Q + Q_LORA
OFF_KR = OFF_KV + KV_LORA
OFF_GLU = OFF_KR + QK_ROPE
OFF_GATE = OFF_GLU + 2 * CONV_WIDTH

D_FF = 2816
N_EXPERTS = 8
D_FF_EXPERT = 3584

EPS = 1e-6
NEG_INF = -1e30

N_PROMPT = BATCH * SEQ
N_SAMPLE = DEC_BATCH * DEC_SEQ
N_TOK = N_PROMPT + N_SAMPLE

LANES = 128
VMEM_BYTES_V7X = 64 * 1024 * 1024

C_POOL = 0
C_Q = C_POOL + POOL_WIDTH
C_KV = C_Q + Q_LORA
C_KRA = C_KV + KV_LORA
C_KRB = C_KRA + LANES
C_GLU = C_KRB + LANES
C_GATE = C_GLU + 2 * CONV_WIDTH
C_END = C_GATE + N_BRANCH * D_MODEL

Q_HEAD = 2 * LANES
Q_COLS = N_HEADS * Q_HEAD
LAT_COLS = N_HEADS * KV_LORA

TM_TOKEN = 512
TM_DENSE = 512
TF_DENSE = 1408
TM_EXPERT = 512
TF_EXPERT = 896
TOP_K = 2
N_PAIRS = TOP_K * N_TOK
EXPERT_TILES = N_PAIRS // TM_EXPERT + N_EXPERTS
M_SORTED = EXPERT_TILES * TM_EXPERT
ROW_TILE = (8, LANES)
DMA_CHUNK = 256
TM_NORM = 1024
TQ = 256
ATTN_ROWS = N_HEADS * TQ
LOG2_E = 1.4426950408889634
Q_SCALE = ATTN_SCALE * LOG2_E
POOL_HALO = 16
CONV_HALO = 32
SEQ_CHUNK = 256
CONV_ROWS = 64
SAMPLE_BB = 16
KEYS_SAMPLE = PAST_LEN + LANES


def _cparams(semantics, vmem_mib):
    assert vmem_mib * 1024 * 1024 < VMEM_BYTES_V7X
    return pltpu.CompilerParams(dimension_semantics=semantics,
                                vmem_limit_bytes=vmem_mib * 1024 * 1024)


def _rms(x, g):
    return x * lax.rsqrt(jnp.mean(x * x, axis=-1, keepdims=True) + EPS) * g


def _dot(a, b):
    return jnp.dot(a, b, preferred_element_type=F32)


def _inproj_kernel(x_ref, g_ref, w_ref, bg_ref, gq_ref, gkv_ref, cos_ref, sin_ref,
                   up_ref, cq_ref, kvb_ref, ckv_ref, kr_ref, z_ref, gate_ref):
    h = _rms(x_ref[...], g_ref[...]).astype(BF16)

    def proj(lo, hi):
        return _dot(h, w_ref[:, lo:hi])

    up_ref[...] = proj(C_POOL, C_Q)
    cq_ref[...] = _rms(proj(C_Q, C_KV), gq_ref[...]).astype(BF16)
    ckv = _rms(proj(C_KV, C_KRA), gkv_ref[...])
    kr = proj(C_KRA, C_KRB) * cos_ref[...] + proj(C_KRB, C_GLU) * sin_ref[...]
    ckv_ref[...] = ckv
    kr_ref[...] = kr[:, :QK_ROPE]
    kvb_ref[:, :KV_LORA] = ckv.astype(BF16)
    kvb_ref[:, KV_LORA:] = kr.astype(BF16)
    glu = proj(C_GLU, C_GATE)
    z_ref[...] = glu[:, :CONV_WIDTH] * jax.nn.sigmoid(glu[:, CONV_WIDTH:])
    for c in range(N_BRANCH):
        lo = c * D_MODEL
        g = proj(C_GATE + lo, C_GATE + lo + D_MODEL) + bg_ref[:, lo:lo + D_MODEL]
        gate_ref[:, lo:lo + D_MODEL] = jax.nn.sigmoid(g).astype(BF16)


def _inproj(x, g, w, bg, gq, gkv, cos, sin):
    tm = TM_TOKEN
    row = lambda i: (i, 0)
    fix = lambda i: (0, 0)
    return pl.pallas_call(
        _inproj_kernel,
        grid=(N_TOK // tm,),
        in_specs=[
            pl.BlockSpec((tm, D_MODEL), row),
            pl.BlockSpec((1, D_MODEL), fix),
            pl.BlockSpec((D_MODEL, C_END), fix),
            pl.BlockSpec((1, N_BRANCH * D_MODEL), fix),
            pl.BlockSpec((1, Q_LORA), fix),
            pl.BlockSpec((1, KV_LORA), fix),
            pl.BlockSpec((tm, LANES), row),
            pl.BlockSpec((tm, LANES), row),
        ],
        out_specs=[
            pl.BlockSpec((tm, POOL_WIDTH), row),
            pl.BlockSpec((tm, Q_LORA), row),
            pl.BlockSpec((tm, Q_HEAD), row),
            pl.BlockSpec((tm, KV_LORA), row),
            pl.BlockSpec((tm, QK_ROPE), row),
            pl.BlockSpec((tm, CONV_WIDTH), row),
            pl.BlockSpec((tm, N_BRANCH * D_MODEL), row),
        ],
        out_shape=[
            jax.ShapeDtypeStruct((N_TOK, POOL_WIDTH), F32),
            jax.ShapeDtypeStruct((N_TOK, Q_LORA), BF16),
            jax.ShapeDtypeStruct((N_TOK, Q_HEAD), BF16),
            jax.ShapeDtypeStruct((N_TOK, KV_LORA), F32),
            jax.ShapeDtypeStruct((N_TOK, QK_ROPE), F32),
            jax.ShapeDtypeStruct((N_TOK, CONV_WIDTH), F32),
            jax.ShapeDtypeStruct((N_TOK, N_BRANCH * D_MODEL), BF16),
        ],
        compiler_params=_cparams(("parallel",), 56),
        name="inproj",
    )(x, g, w, bg, gq, gkv, cos, sin)


def _qproj_kernel(cq_ref, wn_ref, wa_ref, wb_ref, wuk_ref, cos_ref, sin_ref, q_ref):
    cq = cq_ref[...]
    qn = _dot(cq, wn_ref[...]).astype(BF16)
    ra = _dot(cq, wa_ref[...])
    rb = _dot(cq, wb_ref[...])
    cos = cos_ref[...]
    sin = sin_ref[...]
    for h in range(N_HEADS):
        sl = slice(h * LANES, (h + 1) * LANES)
        q_ref[:, h * Q_HEAD:h * Q_HEAD + LANES] = (
            _dot(qn[:, sl], wuk_ref[h]) * Q_SCALE).astype(BF16)
        q_ref[:, h * Q_HEAD + LANES:(h + 1) * Q_HEAD] = (
            (ra[:, sl] * cos + rb[:, sl] * sin) * Q_SCALE).astype(BF16)


def _qproj(cq, wn, wa, wb, wuk, cos, sin):
    tm = TM_TOKEN
    row = lambda i: (i, 0)
    fix = lambda i: (0, 0)
    return pl.pallas_call(
        _qproj_kernel,
        grid=(N_TOK // tm,),
        in_specs=[
            pl.BlockSpec((tm, Q_LORA), row),
            pl.BlockSpec((Q_LORA, N_HEADS * LANES), fix),
            pl.BlockSpec((Q_LORA, N_HEADS * LANES), fix),
            pl.BlockSpec((Q_LORA, N_HEADS * LANES), fix),
            pl.BlockSpec((N_HEADS, LANES, KV_LORA), lambda i: (0, 0, 0)),
            pl.BlockSpec((tm, LANES), row),
            pl.BlockSpec((tm, LANES), row),
        ],
        out_specs=pl.BlockSpec((tm, Q_COLS), row),
        out_shape=jax.ShapeDtypeStruct((N_TOK, Q_COLS), BF16),
        compiler_params=_cparams(("parallel",), 40),
        name="qproj",
    )(cq, wn, wa, wb, wuk, cos, sin)


def _pool_compute(ext_ref, n_rows, cnt_of_window, wbd_ref, sp_ref):
    def ld(j):
        return ext_ref[:, pl.ds(POOL_HALO - j, n_rows), :]

    tok = ld(0)
    run = tok
    sums = {}
    for j in range(1, max(POOL_WINDOWS)):
        run = run + ld(j)
        if j + 1 in POOL_WINDOWS:
            sums[j + 1] = run
    lane = lax.broadcasted_iota(jnp.int32, tok.shape, 2)
    pooled = sums[POOL_WINDOWS[-1]] / cnt_of_window(POOL_WINDOWS[-1])
    for g in range(len(POOL_WINDOWS) - 2, -1, -1):
        w = POOL_WINDOWS[g]
        pooled = jnp.where(lane < (g + 1) * POOL_GROUP, sums[w] / cnt_of_window(w), pooled)
    pooled = (pooled - tok).reshape(-1, POOL_WIDTH).astype(BF16)
    return _dot(pooled, wbd_ref[...]) * sp_ref[...]


def _pool_prompt_kernel(halo_ref, u_ref, wbd_ref, sp_ref, o_ref, ext_ref):
    i = pl.program_id(1)
    ext_ref[:, :POOL_HALO, :] = jnp.where(i > 0, halo_ref[...], 0.0)
    ext_ref[:, POOL_HALO:, :] = u_ref[...]
    pos = lax.broadcasted_iota(jnp.int32, (1, SEQ_CHUNK, 1), 1) + i * SEQ_CHUNK

    def cnt(w):
        return jnp.minimum(pos + 1, w).astype(F32)

    o_ref[0] = _pool_compute(ext_ref, SEQ_CHUNK, cnt, wbd_ref, sp_ref).astype(BF16)


def _pool_prompt(u, wbd, sp):
    hb = SEQ_CHUNK // POOL_HALO
    return pl.pallas_call(
        _pool_prompt_kernel,
        grid=(BATCH, SEQ // SEQ_CHUNK),
        in_specs=[
            pl.BlockSpec((1, POOL_HALO, POOL_WIDTH),
                         lambda b, i: (b, jnp.maximum(i * hb - 1, 0), 0)),
            pl.BlockSpec((1, SEQ_CHUNK, POOL_WIDTH), lambda b, i: (b, i, 0)),
            pl.BlockSpec((POOL_WIDTH, POOL_WIDTH), lambda b, i: (0, 0)),
            pl.BlockSpec((1, POOL_WIDTH), lambda b, i: (0, 0)),
        ],
        out_specs=pl.BlockSpec((1, SEQ_CHUNK, POOL_WIDTH), lambda b, i: (b, i, 0)),
        out_shape=jax.ShapeDtypeStruct((BATCH, SEQ, POOL_WIDTH), BF16),
        scratch_shapes=[pltpu.VMEM((1, POOL_HALO + SEQ_CHUNK, POOL_WIDTH), F32)],
        compiler_params=_cparams(("parallel", "parallel"), 32),
        name="pool_prompt",
    )(u, u, wbd, sp)


def _pool_sample_kernel(ext_ref, wbd_ref, sp_ref, o_ref):
    out = _pool_compute(ext_ref, DEC_SEQ, lambda w: float(w), wbd_ref, sp_ref)
    o_ref[...] = out.reshape(SAMPLE_BB, DEC_SEQ, POOL_WIDTH).astype(BF16)


def _pool_sample(ext, wbd, sp):
    return pl.pallas_call(
        _pool_sample_kernel,
        grid=(DEC_BATCH // SAMPLE_BB,),
        in_specs=[
            pl.BlockSpec((SAMPLE_BB, POOL_HALO + DEC_SEQ, POOL_WIDTH), lambda i: (i, 0, 0)),
            pl.BlockSpec((POOL_WIDTH, POOL_WIDTH), lambda i: (0, 0)),
            pl.BlockSpec((1, POOL_WIDTH), lambda i: (0, 0)),
        ],
        out_specs=pl.BlockSpec((SAMPLE_BB, DEC_SEQ, POOL_WIDTH), lambda i: (i, 0, 0)),
        out_shape=jax.ShapeDtypeStruct((DEC_BATCH, DEC_SEQ, POOL_WIDTH), BF16),
        compiler_params=_cparams(("parallel",), 32),
        name="pool_sample",
    )(ext, wbd, sp)


def _conv_compute(ext_ref, row0, n_rows, wdw_ref, bdw_ref, g_ref, b_ref):
    lead = CONV_HALO - CONV_BUF
    acc = None
    for k in range(CONV_K):
        term = ext_ref[:, pl.ds(row0 + lead + k, n_rows), :] * wdw_ref[k:k + 1, :]
        acc = term if acc is None else acc + term
    zc = acc + bdw_ref[...]
    mu = jnp.mean(zc, axis=-1, keepdims=True)
    xc = zc - mu
    y = xc * lax.rsqrt(jnp.mean(xc * xc, axis=-1, keepdims=True) + EPS)
    y = y * g_ref[...] + b_ref[...]
    return y * jax.nn.sigmoid(y)


def _conv_prompt_kernel(halo_ref, z_ref, wdw_ref, bdw_ref, g_ref, b_ref, o_ref, ext_ref):
    i = pl.program_id(1)
    ext_ref[:, :CONV_HALO, :] = jnp.where(i > 0, halo_ref[...], 0.0)
    ext_ref[:, CONV_HALO:, :] = z_ref[...]
    for r0 in range(0, SEQ_CHUNK, CONV_ROWS):
        o_ref[:, r0:r0 + CONV_ROWS, :] = _conv_compute(
            ext_ref, r0, CONV_ROWS, wdw_ref, bdw_ref, g_ref, b_ref).astype(BF16)


def _conv_prompt(z, wdw, bdw, g, b):
    hb = SEQ_CHUNK // CONV_HALO
    vec = pl.BlockSpec((1, CONV_WIDTH), lambda bb, i: (0, 0))
    return pl.pallas_call(
        _conv_prompt_kernel,
        grid=(BATCH, SEQ // SEQ_CHUNK),
        in_specs=[
            pl.BlockSpec((1, CONV_HALO, CONV_WIDTH),
                         lambda bb, i: (bb, jnp.maximum(i * hb - 1, 0), 0)),
            pl.BlockSpec((1, SEQ_CHUNK, CONV_WIDTH), lambda bb, i: (bb, i, 0)),
            pl.BlockSpec((CONV_K, CONV_WIDTH), lambda bb, i: (0, 0)),
            vec, vec, vec,
        ],
        out_specs=pl.BlockSpec((1, SEQ_CHUNK, CONV_WIDTH), lambda bb, i: (bb, i, 0)),
        out_shape=jax.ShapeDtypeStruct((BATCH, SEQ, CONV_WIDTH), BF16),
        scratch_shapes=[pltpu.VMEM((1, CONV_HALO + SEQ_CHUNK, CONV_WIDTH), F32)],
        compiler_params=_cparams(("parallel", "parallel"), 32),
        name="conv_prompt",
    )(z, z, wdw, bdw, g, b)


def _conv_sample_kernel(ext_ref, wdw_ref, bdw_ref, g_ref, b_ref, o_ref):
    o_ref[...] = _conv_compute(ext_ref, 0, DEC_SEQ, wdw_ref, bdw_ref, g_ref, b_ref).astype(BF16)


def _conv_sample(ext, wdw, bdw, g, b):
    vec = pl.BlockSpec((1, CONV_WIDTH), lambda i: (0, 0))
    return pl.pallas_call(
        _conv_sample_kernel,
        grid=(DEC_BATCH // SAMPLE_BB,),
        in_specs=[
            pl.BlockSpec((SAMPLE_BB, CONV_HALO + DEC_SEQ, CONV_WIDTH), lambda i: (i, 0, 0)),
            pl.BlockSpec((CONV_K, CONV_WIDTH), lambda i: (0, 0)),
            vec, vec, vec,
        ],
        out_specs=pl.BlockSpec((SAMPLE_BB, DEC_SEQ, CONV_WIDTH), lambda i: (i, 0, 0)),
        out_shape=jax.ShapeDtypeStruct((DEC_BATCH, DEC_SEQ, CONV_WIDTH), BF16),
        compiler_params=_cparams(("parallel",), 32),
        name="conv_sample",
    )(ext, wdw, bdw, g, b)


def _attn_prompt_kernel(qt_ref, k_ref, vt_ref, o_ref, m_ref, acc_ref):
    qi = pl.program_id(1)
    qt = qt_ref[0, 0]
    m_ref[...] = jnp.full(m_ref.shape, NEG_INF, F32)
    acc_ref[...] = jnp.zeros(acc_ref.shape, F32)

    def block(j, on_diagonal):
        s = _dot(k_ref[0, j], qt)
        if on_diagonal:
            tok = lax.broadcasted_iota(jnp.int32, s.shape, 1) & (TQ - 1)
            key = lax.broadcasted_iota(jnp.int32, s.shape, 0)
            s = jnp.where(key <= tok, s, NEG_INF)
        m_old = m_ref[...]
        m_new = jnp.maximum(m_old, jnp.max(s, axis=0, keepdims=True))
        p = jnp.exp2(s - m_new).astype(BF16)
        acc_ref[...] = jnp.exp2(m_old - m_new) * acc_ref[...] + _dot(vt_ref[0, j], p)
        m_ref[...] = m_new

    def below_diagonal(j, carry):
        block(j, False)
        return carry

    lax.fori_loop(0, qi, below_diagonal, 0)
    block(qi, True)
    acc = acc_ref[...]
    o_ref[0, 0] = (acc[:KV_LORA] / acc[KV_LORA:]).T.astype(BF16)


def _attn_prompt(qt, k, vt):
    nblk = SEQ // TQ
    return pl.pallas_call(
        _attn_prompt_kernel,
        grid=(BATCH, nblk),
        in_specs=[
            pl.BlockSpec((1, 1, Q_HEAD, ATTN_ROWS), lambda b, i: (b, i, 0, 0)),
            pl.BlockSpec((1, nblk, TQ, Q_HEAD), lambda b, i: (b, 0, 0, 0)),
            pl.BlockSpec((1, nblk, Q_HEAD, TQ), lambda b, i: (b, 0, 0, 0)),
        ],
        out_specs=pl.BlockSpec((1, 1, ATTN_ROWS, KV_LORA), lambda b, i: (b, i, 0, 0)),
        out_shape=jax.ShapeDtypeStruct((BATCH, nblk, ATTN_ROWS, KV_LORA), BF16),
        scratch_shapes=[pltpu.VMEM((1, ATTN_ROWS), F32), pltpu.VMEM((Q_HEAD, ATTN_ROWS), F32)],
        compiler_params=_cparams(("parallel", "arbitrary"), 40),
        name="attn_prompt",
    )(qt, k, vt)


def _attn_sample_kernel(pt_ref, q_ref, kcn_ref, krn_ref, *rest):
    ck_refs = rest[:N_PAGES]
    kr_refs = rest[N_PAGES:2 * N_PAGES]
    o_ref, kc_ref, krt_ref = rest[2 * N_PAGES:]
    del pt_ref

    for p in range(N_PAGES):
        keys = slice(p * PAGE_SIZE, (p + 1) * PAGE_SIZE)
        kc_ref[keys, :] = ck_refs[p][...].astype(BF16)
        krt_ref[:, keys] = kr_refs[p][...].astype(BF16)
    kc_ref[PAST_LEN:, :] = kcn_ref[0]
    krt_ref[:, PAST_LEN:] = krn_ref[0]

    q = q_ref[0]
    s = (lax.dot_general(q[:, :KV_LORA], kc_ref[...], (((1,), (1,)), ((), ())),
                         preferred_element_type=F32)
         + _dot(q[:, KV_LORA:KV_LORA + QK_ROPE], krt_ref[...]))
    t = lax.broadcasted_iota(jnp.int32, s.shape, 0) & (DEC_SEQ - 1)
    key = lax.broadcasted_iota(jnp.int32, s.shape, 1)
    s = jnp.where(key <= t + PAST_LEN, s, NEG_INF)
    m = jnp.max(s, axis=-1, keepdims=True)
    p = jnp.exp2(s - m)
    l = jnp.sum(p, axis=-1, keepdims=True)
    out = _dot(p.astype(BF16), kc_ref[...])
    o_ref[0] = (out / l).astype(BF16)


def _attn_sample(layer, page_table, q, kc_new, krt_new, cache_ckv, cache_krope_t):
    def page_spec(rows, cols, p):
        return pl.BlockSpec((None, None, rows, cols),
                            lambda b, pt: (layer, pt[b * N_PAGES + p], 0, 0))

    grid_spec = pltpu.PrefetchScalarGridSpec(
        num_scalar_prefetch=1,
        grid=(DEC_BATCH,),
        in_specs=(
            [pl.BlockSpec((1, N_HEADS * DEC_SEQ, Q_HEAD), lambda b, pt: (b, 0, 0)),
             pl.BlockSpec((1, LANES, KV_LORA), lambda b, pt: (b, 0, 0)),
             pl.BlockSpec((1, QK_ROPE, LANES), lambda b, pt: (b, 0, 0))]
            + [page_spec(PAGE_SIZE, KV_LORA, p) for p in range(N_PAGES)]
            + [page_spec(QK_ROPE, PAGE_SIZE, p) for p in range(N_PAGES)]),
        out_specs=pl.BlockSpec((1, N_HEADS * DEC_SEQ, KV_LORA), lambda b, pt: (b, 0, 0)),
        scratch_shapes=[pltpu.VMEM((KEYS_SAMPLE, KV_LORA), BF16),
                        pltpu.VMEM((QK_ROPE, KEYS_SAMPLE), BF16)],
    )
    return pl.pallas_call(
        _attn_sample_kernel,
        grid_spec=grid_spec,
        out_shape=jax.ShapeDtypeStruct((DEC_BATCH, N_HEADS * DEC_SEQ, KV_LORA), BF16),
        compiler_params=_cparams(("arbitrary",), 48),
        name="attn_sample",
    )(page_table, q, kc_new, krt_new, *([cache_ckv] * N_PAGES), *([cache_krope_t] * N_PAGES))


def _merge_kernel(with_router, op_ref, lat_ref, oc_ref, gate_ref, x_ref, wuv_ref, wbr_ref,
                  wout_ref, gf_ref, wrh_ref, wrl_ref, xo_ref, h_ref, route_ref):
    o_attn = _dot(lat_ref[...], wuv_ref[...]).astype(BF16)
    br_a = _dot(op_ref[...], wbr_ref[:POOL_WIDTH, :])
    br_b = _dot(o_attn, wbr_ref[POOL_WIDTH:POOL_WIDTH + ATTN_WIDTH, :])
    br_c = _dot(oc_ref[...], wbr_ref[POOL_WIDTH + ATTN_WIDTH:, :])
    merged = (gate_ref[:, :D_MODEL].astype(F32) * br_a
              + gate_ref[:, D_MODEL:2 * D_MODEL].astype(F32) * br_b
              + gate_ref[:, 2 * D_MODEL:].astype(F32) * br_c)
    xn = x_ref[...] + _dot(merged.astype(BF16), wout_ref[...])
    xo_ref[...] = xn
    hn = _rms(xn, gf_ref[...])
    if not with_router:
        h_ref[...] = hn.astype(BF16)
        route_ref[...] = jnp.zeros(route_ref.shape, F32)
        return
    for c in range(ROW_TILE[0]):
        h_ref[:, c, :] = hn[:, c * LANES:(c + 1) * LANES]
    hh = hn.astype(BF16)
    hl = (hn - hh.astype(F32)).astype(BF16)
    lg = _dot(hh, wrh_ref[...]) + _dot(hl, wrh_ref[...]) + _dot(hh, wrl_ref[...])
    lane = lax.broadcasted_iota(jnp.int32, lg.shape, 1).astype(F32)
    lg = jnp.where(lane < N_EXPERTS, lg, NEG_INF)
    m1 = jnp.max(lg, axis=-1, keepdims=True)
    i1 = jnp.min(jnp.where(lg == m1, lane, float(LANES)), axis=-1, keepdims=True)
    lg2 = jnp.where(lane == i1, NEG_INF, lg)
    m2 = jnp.max(lg2, axis=-1, keepdims=True)
    i2 = jnp.min(jnp.where(lg2 == m2, lane, float(LANES)), axis=-1, keepdims=True)
    e = jnp.exp(m2 - m1)
    w1 = 1.0 / (1.0 + e)
    w2 = e / (1.0 + e)
    route_ref[...] = jnp.where(lane == 0, i1, jnp.where(lane == 1, i2, jnp.where(
        lane == 2, w1, jnp.where(lane == 3, w2, 0.0))))


def _merge(with_router, o_pool, lat, o_conv, gates, x, wuv, wbr, wout, gf, wrh, wrl):
    tm = TM_TOKEN
    row = lambda i: (i, 0)
    fix = lambda i: (0, 0)
    if with_router:
        h_spec = pl.BlockSpec((tm,) + ROW_TILE, lambda i: (i, 0, 0))
        h_shape = jax.ShapeDtypeStruct((N_TOK,) + ROW_TILE, F32)
    else:
        h_spec = pl.BlockSpec((tm, D_MODEL), row)
        h_shape = jax.ShapeDtypeStruct((N_TOK, D_MODEL), BF16)
    return pl.pallas_call(
        functools.partial(_merge_kernel, with_router),
        grid=(N_TOK // tm,),
        in_specs=[
            pl.BlockSpec((tm, POOL_WIDTH), row),
            pl.BlockSpec((tm, LAT_COLS), row),
            pl.BlockSpec((tm, CONV_WIDTH), row),
            pl.BlockSpec((tm, N_BRANCH * D_MODEL), row),
            pl.BlockSpec((tm, D_MODEL), row),
            pl.BlockSpec((LAT_COLS, ATTN_WIDTH), fix),
            pl.BlockSpec((D_MODEL, D_MODEL), fix),
            pl.BlockSpec((D_MODEL, D_MODEL), fix),
            pl.BlockSpec((1, D_MODEL), fix),
            pl.BlockSpec((D_MODEL, LANES), fix),
            pl.BlockSpec((D_MODEL, LANES), fix),
        ],
        out_specs=[
            pl.BlockSpec((tm, D_MODEL), row),
            h_spec,
            pl.BlockSpec((tm, LANES), row),
        ],
        out_shape=[
            jax.ShapeDtypeStruct((N_TOK, D_MODEL), F32),
            h_shape,
            jax.ShapeDtypeStruct((N_TOK, LANES), F32),
        ],
        compiler_params=_cparams(("parallel",), 48),
        name="merge_router" if with_router else "merge",
    )(o_pool, lat, o_conv, gates, x, wuv, wbr, wout, gf, wrh, wrl)


def _row_chunk_copy(src_ref, dst_ref, sem):
    return pltpu.make_async_copy(src_ref.at[pl.ds(0, DMA_CHUNK)], dst_ref.at[pl.ds(0, DMA_CHUNK)], sem)


def _dispatch_kernel(pos_ref, h_ref, init_ref, xs_ref, sem):
    del init_ref
    i = pl.program_id(0)
    n = pl.num_programs(0)
    pair0 = i * DMA_CHUNK
    tok0 = pair0 - jnp.where(pair0 >= N_TOK, N_TOK, 0)

    def issue(r, carry):
        pltpu.make_async_copy(h_ref.at[tok0 + r], xs_ref.at[pos_ref[pair0 + r]],
                              sem.at[i % 2]).start()
        return carry

    lax.fori_loop(0, DMA_CHUNK, issue, 0, unroll=8)

    @pl.when(i > 0)
    def _():
        _row_chunk_copy(h_ref, xs_ref, sem.at[(i + 1) % 2]).wait()

    @pl.when(i == n - 1)
    def _():
        _row_chunk_copy(h_ref, xs_ref, sem.at[i % 2]).wait()


def _dispatch(pos, h_rows, init):
    grid_spec = pltpu.PrefetchScalarGridSpec(
        num_scalar_prefetch=1,
        grid=(N_PAIRS // DMA_CHUNK,),
        in_specs=[pl.BlockSpec(memory_space=pl.ANY), pl.BlockSpec(memory_space=pl.ANY)],
        out_specs=pl.BlockSpec(memory_space=pl.ANY),
        scratch_shapes=[pltpu.SemaphoreType.DMA((2,))],
    )
    return pl.pallas_call(
        _dispatch_kernel,
        grid_spec=grid_spec,
        out_shape=jax.ShapeDtypeStruct((M_SORTED,) + ROW_TILE, F32),
        input_output_aliases={2: 0},
        compiler_params=_cparams(("arbitrary",), 16),
        name="dispatch",
    )(pos, h_rows, init)


def _expert_ffn_kernel(te_ref, na_ref, xs_ref, wg_ref, wu_ref, wd_ref, ys_ref, xb_ref, acc_ref):
    del te_ref
    i = pl.program_id(0)
    f = pl.program_id(1)
    active = i < na_ref[0]

    @pl.when(active & (f == 0))
    def _():
        xb_ref[...] = jnp.concatenate(
            [xs_ref[:, c, :] for c in range(ROW_TILE[0])], axis=-1).astype(BF16)

    @pl.when(active)
    def _():
        x = xb_ref[...]
        a = _dot(x, wg_ref[...])
        b = _dot(x, wu_ref[...])
        y = _dot((a * jax.nn.sigmoid(a) * b).astype(BF16), wd_ref[...])

        @pl.when(f == 0)
        def _():
            acc_ref[...] = y

        @pl.when(f > 0)
        def _():
            acc_ref[...] += y

    @pl.when(f == pl.num_programs(1) - 1)
    def _():
        y = jnp.where(active, acc_ref[...], 0.0)
        for c in range(ROW_TILE[0]):
            ys_ref[:, c, :] = y[:, c * LANES:(c + 1) * LANES]


def _expert_ffn(tile_expert, n_active, xs, wg, wu, wd):
    tm, tf = TM_EXPERT, TF_EXPERT
    nf = D_FF_EXPERT // tf

    def tile(i, na):
        return jnp.minimum(i, na[0] - 1)

    def hidden(i, f, na):
        return jnp.where(i < na[0], f, nf - 1)

    grid_spec = pltpu.PrefetchScalarGridSpec(
        num_scalar_prefetch=2,
        grid=(EXPERT_TILES, nf),
        in_specs=[
            pl.BlockSpec((tm,) + ROW_TILE, lambda i, f, te, na: (tile(i, na), 0, 0)),
            pl.BlockSpec((None, D_MODEL, tf),
                         lambda i, f, te, na: (te[tile(i, na)], 0, hidden(i, f, na))),
            pl.BlockSpec((None, D_MODEL, tf),
                         lambda i, f, te, na: (te[tile(i, na)], 0, hidden(i, f, na))),
            pl.BlockSpec((None, tf, D_MODEL),
                         lambda i, f, te, na: (te[tile(i, na)], hidden(i, f, na), 0)),
        ],
        out_specs=pl.BlockSpec((tm,) + ROW_TILE, lambda i, f, te, na: (i, 0, 0)),
        scratch_shapes=[pltpu.VMEM((tm, D_MODEL), BF16), pltpu.VMEM((tm, D_MODEL), F32)],
    )
    return pl.pallas_call(
        _expert_ffn_kernel,
        grid_spec=grid_spec,
        out_shape=jax.ShapeDtypeStruct((M_SORTED,) + ROW_TILE, F32),
        compiler_params=_cparams(("arbitrary", "arbitrary"), 48),
        name="expert_ffn",
    )(tile_expert, n_active, xs, wg, wu, wd)


def _combine_kernel(pos_ref, ys_ref, x_ref, route_ref, o_ref, buf_ref, sem):
    i = pl.program_id(0)
    n = pl.num_programs(0)
    tmc = DMA_CHUNK // TOP_K

    def issue(tile, slot):
        def body(r, carry):
            for k in range(TOP_K):
                pltpu.make_async_copy(ys_ref.at[pos_ref[k * N_TOK + tile * tmc + r]],
                                      buf_ref.at[slot, k * tmc + r], sem.at[slot]).start()
            return carry

        lax.fori_loop(0, tmc, body, 0, unroll=8)

    @pl.when(i == 0)
    def _():
        issue(0, 0)

    @pl.when(i + 1 < n)
    def _():
        issue(i + 1, (i + 1) % 2)

    slot = i % 2
    _row_chunk_copy(ys_ref, buf_ref.at[slot], sem.at[slot]).wait()
    w1 = route_ref[:, 2:3]
    w2 = route_ref[:, 3:4]
    for c in range(ROW_TILE[0]):
        cols = slice(c * LANES, (c + 1) * LANES)
        o_ref[:, cols] = x_ref[:, cols] + (w1 * buf_ref[slot, :tmc, c, :]
                                            + w2 * buf_ref[slot, tmc:, c, :])


def _combine(pos, ys, x, route):
    tmc = DMA_CHUNK // TOP_K
    grid_spec = pltpu.PrefetchScalarGridSpec(
        num_scalar_prefetch=1,
        grid=(N_TOK // tmc,),
        in_specs=[
            pl.BlockSpec(memory_space=pl.ANY),
            pl.BlockSpec((tmc, D_MODEL), lambda i, pos: (i, 0)),
            pl.BlockSpec((tmc, LANES), lambda i, pos: (i, 0)),
        ],
        out_specs=pl.BlockSpec((tmc, D_MODEL), lambda i, pos: (i, 0)),
        scratch_shapes=[pltpu.VMEM((2, DMA_CHUNK) + ROW_TILE, F32),
                        pltpu.SemaphoreType.DMA((2,))],
    )
    return pl.pallas_call(
        _combine_kernel,
        grid_spec=grid_spec,
        out_shape=jax.ShapeDtypeStruct((N_TOK, D_MODEL), F32),
        compiler_params=_cparams(("arbitrary",), 32),
        name="combine",
    )(pos, ys, x, route)


def _route(route):
    experts = jnp.concatenate([route[:, 0], route[:, 1]]).astype(jnp.int32)
    one_hot = (experts[:, None] == jnp.arange(N_EXPERTS)[None, :]).astype(jnp.int32)
    running = jnp.cumsum(one_hot, axis=0)
    rank = jnp.sum(one_hot * running, axis=1) - 1
    tiles = (running[-1] + TM_EXPERT - 1) // TM_EXPERT
    tile_end = jnp.cumsum(tiles)
    pos = jnp.sum(one_hot * ((tile_end - tiles) * TM_EXPERT)[None, :], axis=1) + rank
    tile_ids = jnp.arange(EXPERT_TILES, dtype=jnp.int32)
    tile_expert = jnp.minimum(jnp.sum(tile_ids[:, None] >= tile_end[None, :], axis=1),
                              N_EXPERTS - 1)
    return pos.astype(jnp.int32), tile_expert.astype(jnp.int32), tile_end[-1:].astype(jnp.int32)


def _ffn_kernel(h_ref, wg_ref, wu_ref, wd_ref, x_ref, o_ref, acc_ref):
    f = pl.program_id(1)
    h = h_ref[...]
    a = _dot(h, wg_ref[...])
    b = _dot(h, wu_ref[...])
    y = _dot((a * jax.nn.sigmoid(a) * b).astype(BF16), wd_ref[...])

    @pl.when(f == 0)
    def _():
        acc_ref[...] = y

    @pl.when(f > 0)
    def _():
        acc_ref[...] += y

    @pl.when(f == pl.num_programs(1) - 1)
    def _():
        o_ref[...] = x_ref[...] + acc_ref[...]


def _ffn(h, wg, wu, wd, x):
    tm, tf = TM_DENSE, TF_DENSE
    row = lambda i, f: (i, 0)
    return pl.pallas_call(
        _ffn_kernel,
        grid=(N_TOK // tm, D_FF // tf),
        in_specs=[
            pl.BlockSpec((tm, D_MODEL), row),
            pl.BlockSpec((D_MODEL, tf), lambda i, f: (0, f)),
            pl.BlockSpec((D_MODEL, tf), lambda i, f: (0, f)),
            pl.BlockSpec((tf, D_MODEL), lambda i, f: (f, 0)),
            pl.BlockSpec((tm, D_MODEL), row),
        ],
        out_specs=pl.BlockSpec((tm, D_MODEL), row),
        out_shape=jax.ShapeDtypeStruct((N_TOK, D_MODEL), F32),
        scratch_shapes=[pltpu.VMEM((tm, D_MODEL), F32)],
        compiler_params=_cparams(("parallel", "arbitrary"), 56),
        name="ffn",
    )(h, wg, wu, wd, x)


def _final_norm_kernel(x_ref, g_ref, o_ref):
    o_ref[...] = _rms(x_ref[...], g_ref[...])


def _final_norm(x, g):
    tm = TM_NORM
    return pl.pallas_call(
        _final_norm_kernel,
        grid=(N_TOK // tm,),
        in_specs=[pl.BlockSpec((tm, D_MODEL), lambda i: (i, 0)),
                  pl.BlockSpec((1, D_MODEL), lambda i: (0, 0))],
        out_specs=pl.BlockSpec((tm, D_MODEL), lambda i: (i, 0)),
        out_shape=jax.ShapeDtypeStruct((N_TOK, D_MODEL), F32),
        compiler_params=_cparams(("parallel",), 32),
        name="final_norm",
    )(x, g)


def _pad_cols(w, width):
    return jnp.pad(w, ((0, 0), (0, width - w.shape[1])))


def _swap_halves(w):
    half = QK_ROPE // 2
    return jnp.concatenate([w[..., half:], w[..., :half]], axis=-1)


def _prep_w_in(w):
    kr = w[:, OFF_KR:OFF_GLU]
    return jnp.concatenate([
        w[:, OFF_POOL:OFF_KR],
        _pad_cols(kr, LANES), _pad_cols(_swap_halves(kr), LANES),
        w[:, OFF_GLU:]], axis=1).astype(BF16)


def _prep_w_uq(w_uq):
    def per_head(part):
        pad = LANES - part.shape[-1]
        return jnp.pad(part, ((0, 0), (0, 0), (0, pad))).reshape(Q_LORA, N_HEADS * LANES).astype(BF16)

    rope = w_uq[..., QK_NOPE:]
    return per_head(w_uq[..., :QK_NOPE]), per_head(rope), per_head(_swap_halves(rope))


def _prep_w_uk(w_uk):
    wt = jnp.transpose(w_uk, (1, 2, 0))
    return jnp.pad(wt, ((0, 0), (0, LANES - QK_NOPE), (0, 0))).astype(BF16)


def _prep_w_uv(w_uv):
    eye = jnp.eye(N_HEADS, dtype=w_uv.dtype)
    wbd = jnp.einsum('rhd,hg->hrgd', w_uv, eye)
    return wbd.reshape(LAT_COLS, ATTN_WIDTH).astype(BF16)


def _prep_w_pool(w_pool):
    n = len(POOL_WINDOWS)
    eye = jnp.eye(n, dtype=w_pool.dtype)
    return jnp.einsum('gcd,gk->gckd', w_pool, eye).reshape(POOL_WIDTH, POOL_WIDTH).astype(BF16)


def _rope_tables():
    inv = ROPE_THETA ** (-jnp.arange(0, QK_ROPE, 2, dtype=F32) / QK_ROPE)
    pos = jnp.concatenate([jnp.tile(jnp.arange(SEQ), BATCH),
                           jnp.tile(PAST_LEN + jnp.arange(DEC_SEQ), DEC_BATCH)])
    ang = pos.astype(F32)[:, None] * inv[None, :]
    cos, sin = jnp.cos(ang), jnp.sin(ang)
    cos_t = _pad_cols(jnp.concatenate([cos, cos], axis=1), LANES)
    sin_t = _pad_cols(jnp.concatenate([-sin, sin], axis=1), LANES)
    return cos_t, sin_t


def _split_hi_lo(w):
    hi = w.astype(BF16)
    lo = (w - hi.astype(F32)).astype(BF16)
    return hi, lo


def kernel(x_prompt, x_sample, cache_ckv, cache_krope, page_table, state_pool, state_conv, g_mix_norm, w_in, b_gate, w_pool, s_pool, g_q_lat, w_uq, g_kv_lat, w_uk, w_uv, w_dw, b_dw, g_conv_ln, b_conv_ln, w_br, w_out, g_ffn_norm, w_d_gate, w_d_up, w_d_down, w_router, w_e_gate, w_e_up, w_e_down, g_final):
    x = jnp.concatenate([x_prompt.reshape(N_PROMPT, D_MODEL),
                         x_sample.reshape(N_SAMPLE, D_MODEL)], axis=0)
    cos_t, sin_t = _rope_tables()
    pt_flat = page_table.reshape(-1)
    cache_krope_t = jnp.swapaxes(cache_krope, 2, 3)

    ckv_p, kr_p, pool_p, conv_p = [], [], [], []
    ckv_s, kr_s, pool_s, conv_s = [], [], [], []
    for l in range(DEPTH):
        u_pool, c_q, kvb, c_kv, k_r, z, gates = _inproj(
            x, g_mix_norm[l][None], _prep_w_in(w_in[l]), b_gate[l][None],
            g_q_lat[l][None], g_kv_lat[l][None], cos_t, sin_t)

        wn, wa, wb = _prep_w_uq(w_uq[l])
        q = _qproj(c_q, wn, wa, wb, _prep_w_uk(w_uk[l]), cos_t, sin_t)

        wbd = _prep_w_pool(w_pool[l])
        sp = s_pool[l][None]
        u_p = u_pool[:N_PROMPT].reshape(BATCH, SEQ, POOL_WIDTH)
        u_s = u_pool[N_PROMPT:].reshape(DEC_BATCH, DEC_SEQ, POOL_WIDTH)
        pool_ext = jnp.concatenate(
            [jnp.zeros((DEC_BATCH, POOL_HALO - POOL_BUF, POOL_WIDTH), F32), state_pool[l], u_s], axis=1)
        o_pool = jnp.concatenate([
            _pool_prompt(u_p, wbd, sp).reshape(N_PROMPT, POOL_WIDTH),
            _pool_sample(pool_ext, wbd, sp).reshape(N_SAMPLE, POOL_WIDTH)], axis=0)

        z_p = z[:N_PROMPT].reshape(BATCH, SEQ, CONV_WIDTH)
        z_s = z[N_PROMPT:].reshape(DEC_BATCH, DEC_SEQ, CONV_WIDTH)
        conv_ext = jnp.concatenate(
            [jnp.zeros((DEC_BATCH, CONV_HALO - CONV_BUF, CONV_WIDTH), F32), state_conv[l], z_s], axis=1)
        conv_args = (w_dw[l], b_dw[l][None], g_conv_ln[l][None], b_conv_ln[l][None])
        o_conv = jnp.concatenate([
            _conv_prompt(z_p, *conv_args).reshape(N_PROMPT, CONV_WIDTH),
            _conv_sample(conv_ext, *conv_args).reshape(N_SAMPLE, CONV_WIDTH)], axis=0)

        nblk = SEQ // TQ
        kv_p = kvb[:N_PROMPT].reshape(BATCH, nblk, TQ, Q_HEAD)
        vt_p = jnp.concatenate([jnp.swapaxes(kv_p[..., :KV_LORA], 2, 3),
                                jnp.ones((BATCH, nblk, Q_HEAD - KV_LORA, TQ), BF16)], axis=2)
        qt_p = q[:N_PROMPT].reshape(BATCH, nblk, TQ, N_HEADS, Q_HEAD)
        qt_p = jnp.transpose(qt_p, (0, 1, 4, 3, 2)).reshape(BATCH, nblk, Q_HEAD, ATTN_ROWS)
        lat_p = _attn_prompt(qt_p, kv_p, vt_p).reshape(BATCH, nblk, N_HEADS, TQ, KV_LORA)
        lat_p = jnp.swapaxes(lat_p, 2, 3)
        q_s = q[N_PROMPT:].reshape(DEC_BATCH, DEC_SEQ, N_HEADS, Q_HEAD)
        q_s = jnp.swapaxes(q_s, 1, 2).reshape(DEC_BATCH, N_HEADS * DEC_SEQ, Q_HEAD)
        kv_s = kvb[N_PROMPT:].reshape(DEC_BATCH, DEC_SEQ, Q_HEAD)
        kc_new = jnp.pad(kv_s[..., :KV_LORA], ((0, 0), (0, LANES - DEC_SEQ), (0, 0)))
        krt_new = jnp.pad(jnp.swapaxes(kv_s[..., KV_LORA:KV_LORA + QK_ROPE], 1, 2),
                          ((0, 0), (0, 0), (0, LANES - DEC_SEQ)))
        lat_s = _attn_sample(l, pt_flat, q_s, kc_new, krt_new, cache_ckv, cache_krope_t)
        lat_s = jnp.swapaxes(lat_s.reshape(DEC_BATCH, N_HEADS, DEC_SEQ, KV_LORA), 1, 2)
        lat = jnp.concatenate([lat_p.reshape(N_PROMPT, LAT_COLS),
                               lat_s.reshape(N_SAMPLE, LAT_COLS)], axis=0)

        is_moe = l % 2 == 1
        wr = _pad_cols(w_router[l // 2], LANES) if is_moe else jnp.zeros((D_MODEL, LANES), F32)
        wrh, wrl = _split_hi_lo(wr)
        x, h, route = _merge(is_moe, o_pool, lat, o_conv, gates, x, _prep_w_uv(w_uv[l]),
                             w_br[l].astype(BF16), w_out[l].astype(BF16), g_ffn_norm[l][None],
                             wrh, wrl)
        if is_moe:
            pos, tile_expert, n_active = _route(route)
            xs = _dispatch(pos, h, jnp.zeros((M_SORTED,) + ROW_TILE, F32))
            ys = _expert_ffn(tile_expert, n_active, xs, w_e_gate[l // 2].astype(BF16),
                             w_e_up[l // 2].astype(BF16), w_e_down[l // 2].astype(BF16))
            x = _combine(pos, ys, x, route)
        else:
            x = _ffn(h, w_d_gate[l // 2].astype(BF16), w_d_up[l // 2].astype(BF16),
                     w_d_down[l // 2].astype(BF16), x)

        ckv_p.append(c_kv[:N_PROMPT].reshape(BATCH, SEQ, KV_LORA))
        kr_p.append(k_r[:N_PROMPT].reshape(BATCH, SEQ, QK_ROPE))
        pool_p.append(u_p[:, -POOL_BUF:])
        conv_p.append(z_p[:, -CONV_BUF:])
        ckv_s.append(c_kv[N_PROMPT:].reshape(DEC_BATCH, DEC_SEQ, KV_LORA))
        kr_s.append(k_r[N_PROMPT:].reshape(DEC_BATCH, DEC_SEQ, QK_ROPE))
        pool_s.append(jnp.concatenate([state_pool[l], u_s], axis=1)[:, -POOL_BUF:])
        conv_s.append(jnp.concatenate([state_conv[l], z_s], axis=1)[:, -CONV_BUF:])

    y = _final_norm(x, g_final[None])
    return (y[:N_PROMPT].reshape(BATCH, SEQ, D_MODEL),
            y[N_PROMPT:].reshape(DEC_BATCH, DEC_SEQ, D_MODEL),
            jnp.stack(ckv_p), jnp.stack(kr_p), jnp.stack(pool_p), jnp.stack(conv_p),
            jnp.stack(ckv_s), jnp.stack(kr_s), jnp.stack(pool_s), jnp.stack(conv_s))
```

```python
import functools

import jax
import jax.numpy as jnp
from jax import lax
from jax.experimental import pallas as pl
from jax.experimental.pallas import tpu as pltpu

F32 = jnp.float32
BF16 = jnp.bfloat16

D_MODEL = 1024
BATCH = 8
SEQ = 2048
DEPTH = 4
DEC_BATCH = 128
DEC_SEQ = 8
PAST_LEN = 8192
PAGE_SIZE = 128
N_PAGES = PAST_LEN // PAGE_SIZE

POOL_WINDOWS = (2, 4, 8, 16)
POOL_GROUP = 64
POOL_WIDTH = 256
POOL_BUF = 15

N_HEADS = 8
QK_NOPE = 64
QK_ROPE = 32
V_DIM = 64
Q_LORA = 256
KV_LORA = 128
ROPE_THETA = 10000.0
ATTN_WIDTH = N_HEADS * V_DIM
ATTN_SCALE = (QK_NOPE + QK_ROPE) ** -0.5

CONV_WIDTH = 256
CONV_K = 31
CONV_BUF = CONV_K - 1

N_BRANCH = 3
OFF_POOL = 0
OFF_Q = OFF_POOL + POOL_WIDTH
OFF_KV = OFF_Q + Q_LORA
OFF_KR = OFF_KV + KV_LORA
OFF_GLU = OFF_KR + QK_ROPE
OFF_GATE = OFF_GLU + 2 * CONV_WIDTH

D_FF = 2816
N_EXPERTS = 8
D_FF_EXPERT = 3584

EPS = 1e-6
NEG_INF = -1e30

N_PROMPT = BATCH * SEQ
N_SAMPLE = DEC_BATCH * DEC_SEQ
N_TOK = N_PROMPT + N_SAMPLE

LANES = 128
VMEM_BYTES_V7X = 64 * 1024 * 1024

C_POOL = 0
C_Q = C_POOL + POOL_WIDTH
C_KV = C_Q + Q_LORA
C_KRA = C_KV + KV_LORA
C_KRB = C_KRA + LANES
C_GLU = C_KRB + LANES
C_GATE = C_GLU + 2 * CONV_WIDTH
C_END = C_GATE + N_BRANCH * D_MODEL

Q_HEAD = 2 * LANES
Q_COLS = N_HEADS * Q_HEAD
LAT_COLS = N_HEADS * KV_LORA

TM_TOKEN = 512
TM_DENSE = 512
TF_DENSE = 1408
TM_EXPERT = 512
TF_EXPERT = 896
TOP_K = 2
N_PAIRS = TOP_K * N_TOK
EXPERT_TILES = N_PAIRS // TM_EXPERT + N_EXPERTS
M_SORTED = EXPERT_TILES * TM_EXPERT
ROW_TILE = (8, LANES)
DMA_CHUNK = 256
TM_NORM = 1024
TQ = 256
ATTN_ROWS = N_HEADS * TQ
LOG2_E = 1.4426950408889634
Q_SCALE = ATTN_SCALE * LOG2_E
POOL_HALO = 16
CONV_HALO = 32
SEQ_CHUNK = 256
CONV_ROWS = 64
SAMPLE_BB = 16


def _cparams(semantics, vmem_mib):
    assert vmem_mib * 1024 * 1024 < VMEM_BYTES_V7X
    return pltpu.CompilerParams(dimension_semantics=semantics,
                                vmem_limit_bytes=vmem_mib * 1024 * 1024)


def _rms(x, g):
    return x * lax.rsqrt(jnp.mean(x * x, axis=-1, keepdims=True) + EPS) * g


def _dot(a, b):
    return jnp.dot(a, b, preferred_element_type=F32)


def _dot_nt(a, b):
    return lax.dot_general(a, b, (((1,), (1,)), ((), ())), preferred_element_type=F32)


def _inproj_kernel(x_ref, g_ref, w_ref, bg_ref, gq_ref, gkv_ref, cos_ref, sin_ref,
                   up_ref, cq_ref, kvb_ref, ckv_ref, kr_ref, z_ref, gate_ref):
    h = _rms(x_ref[...], g_ref[...]).astype(BF16)

    def proj(lo, hi):
        return _dot(h, w_ref[:, lo:hi])

    up_ref[...] = proj(C_POOL, C_Q)
    cq_ref[...] = _rms(proj(C_Q, C_KV), gq_ref[...]).astype(BF16)
    ckv = _rms(proj(C_KV, C_KRA), gkv_ref[...])
    kr = proj(C_KRA, C_KRB) * cos_ref[...] + proj(C_KRB, C_GLU) * sin_ref[...]
    ckv_ref[...] = ckv
    kr_ref[...] = kr[:, :QK_ROPE]
    kvb_ref[:, :KV_LORA] = ckv.astype(BF16)
    kvb_ref[:, KV_LORA:] = kr.astype(BF16)
    glu = proj(C_GLU, C_GATE)
    z_ref[...] = glu[:, :CONV_WIDTH] * jax.nn.sigmoid(glu[:, CONV_WIDTH:])
    for c in range(N_BRANCH):
        lo = c * D_MODEL
        g = proj(C_GATE + lo, C_GATE + lo + D_MODEL) + bg_ref[:, lo:lo + D_MODEL]
        gate_ref[:, lo:lo + D_MODEL] = jax.nn.sigmoid(g).astype(BF16)


def _inproj(x, g, w, bg, gq, gkv, cos, sin):
    tm = TM_TOKEN
    row = lambda i: (i, 0)
    fix = lambda i: (0, 0)
    return pl.pallas_call(
        _inproj_kernel,
        grid=(N_TOK // tm,),
        in_specs=[
            pl.BlockSpec((tm, D_MODEL), row),
            pl.BlockSpec((1, D_MODEL), fix),
            pl.BlockSpec((D_MODEL, C_END), fix),
            pl.BlockSpec((1, N_BRANCH * D_MODEL), fix),
            pl.BlockSpec((1, Q_LORA), fix),
            pl.BlockSpec((1, KV_LORA), fix),
            pl.BlockSpec((tm, LANES), row),
            pl.BlockSpec((tm, LANES), row),
        ],
        out_specs=[
            pl.BlockSpec((tm, POOL_WIDTH), row),
            pl.BlockSpec((tm, Q_LORA), row),
            pl.BlockSpec((tm, Q_HEAD), row),
            pl.BlockSpec((tm, KV_LORA), row),
            pl.BlockSpec((tm, QK_ROPE), row),
            pl.BlockSpec((tm, CONV_WIDTH), row),
            pl.BlockSpec((tm, N_BRANCH * D_MODEL), row),
        ],
        out_shape=[
            jax.ShapeDtypeStruct((N_TOK, POOL_WIDTH), F32),
            jax.ShapeDtypeStruct((N_TOK, Q_LORA), BF16),
            jax.ShapeDtypeStruct((N_TOK, Q_HEAD), BF16),
            jax.ShapeDtypeStruct((N_TOK, KV_LORA), F32),
            jax.ShapeDtypeStruct((N_TOK, QK_ROPE), F32),
            jax.ShapeDtypeStruct((N_TOK, CONV_WIDTH), F32),
            jax.ShapeDtypeStruct((N_TOK, N_BRANCH * D_MODEL), BF16),
        ],
        compiler_params=_cparams(("parallel",), 56),
        name="inproj",
    )(x, g, w, bg, gq, gkv, cos, sin)


def _qproj_kernel(cq_ref, wn_ref, wa_ref, wb_ref, wuk_ref, cos_ref, sin_ref, q_ref):
    cq = cq_ref[...]
    qn = _dot(cq, wn_ref[...]).astype(BF16)
    ra = _dot(cq, wa_ref[...])
    rb = _dot(cq, wb_ref[...])
    cos = cos_ref[...]
    sin = sin_ref[...]
    for h in range(N_HEADS):
        sl = slice(h * LANES, (h + 1) * LANES)
        q_ref[:, h * Q_HEAD:h * Q_HEAD + LANES] = (
            _dot(qn[:, sl], wuk_ref[h]) * Q_SCALE).astype(BF16)
        q_ref[:, h * Q_HEAD + LANES:(h + 1) * Q_HEAD] = (
            (ra[:, sl] * cos + rb[:, sl] * sin) * Q_SCALE).astype(BF16)


def _qproj(cq, wn, wa, wb, wuk, cos, sin):
    tm = TM_TOKEN
    row = lambda i: (i, 0)
    fix = lambda i: (0, 0)
    return pl.pallas_call(
        _qproj_kernel,
        grid=(N_TOK // tm,),
        in_specs=[
            pl.BlockSpec((tm, Q_LORA), row),
            pl.BlockSpec((Q_LORA, N_HEADS * LANES), fix),
            pl.BlockSpec((Q_LORA, N_HEADS * LANES), fix),
            pl.BlockSpec((Q_LORA, N_HEADS * LANES), fix),
            pl.BlockSpec((N_HEADS, LANES, KV_LORA), lambda i: (0, 0, 0)),
            pl.BlockSpec((tm, LANES), row),
            pl.BlockSpec((tm, LANES), row),
        ],
        out_specs=pl.BlockSpec((tm, Q_COLS), row),
        out_shape=jax.ShapeDtypeStruct((N_TOK, Q_COLS), BF16),
        compiler_params=_cparams(("parallel",), 40),
        name="qproj",
    )(cq, wn, wa, wb, wuk, cos, sin)


def _pool_compute(ext_ref, n_rows, cnt_of_window, wbd_ref, sp_ref):
    def ld(j):
        return ext_ref[:, pl.ds(POOL_HALO - j, n_rows), :]

    tok = ld(0)
    run = tok
    sums = {}
    for j in range(1, max(POOL_WINDOWS)):
        run = run + ld(j)
        if j + 1 in POOL_WINDOWS:
            sums[j + 1] = run
    lane = lax.broadcasted_iota(jnp.int32, tok.shape, 2)
    pooled = sums[POOL_WINDOWS[-1]] / cnt_of_window(POOL_WINDOWS[-1])
    for g in range(len(POOL_WINDOWS) - 2, -1, -1):
        w = POOL_WINDOWS[g]
        pooled = jnp.where(lane < (g + 1) * POOL_GROUP, sums[w] / cnt_of_window(w), pooled)
    pooled = (pooled - tok).reshape(-1, POOL_WIDTH).astype(BF16)
    return _dot(pooled, wbd_ref[...]) * sp_ref[...]


def _pool_prompt_kernel(halo_ref, u_ref, wbd_ref, sp_ref, o_ref, ext_ref):
    i = pl.program_id(1)
    ext_ref[:, :POOL_HALO, :] = jnp.where(i > 0, halo_ref[...], 0.0)
    ext_ref[:, POOL_HALO:, :] = u_ref[...]
    pos = lax.broadcasted_iota(jnp.int32, (1, SEQ_CHUNK, 1), 1) + i * SEQ_CHUNK

    def cnt(w):
        return jnp.minimum(pos + 1, w).astype(F32)

    o_ref[0] = _pool_compute(ext_ref, SEQ_CHUNK, cnt, wbd_ref, sp_ref).astype(BF16)


def _pool_prompt(u, wbd, sp):
    hb = SEQ_CHUNK // POOL_HALO
    return pl.pallas_call(
        _pool_prompt_kernel,
        grid=(BATCH, SEQ // SEQ_CHUNK),
        in_specs=[
            pl.BlockSpec((1, POOL_HALO, POOL_WIDTH),
                         lambda b, i: (b, jnp.maximum(i * hb - 1, 0), 0)),
            pl.BlockSpec((1, SEQ_CHUNK, POOL_WIDTH), lambda b, i: (b, i, 0)),
            pl.BlockSpec((POOL_WIDTH, POOL_WIDTH), lambda b, i: (0, 0)),
            pl.BlockSpec((1, POOL_WIDTH), lambda b, i: (0, 0)),
        ],
        out_specs=pl.BlockSpec((1, SEQ_CHUNK, POOL_WIDTH), lambda b, i: (b, i, 0)),
        out_shape=jax.ShapeDtypeStruct((BATCH, SEQ, POOL_WIDTH), BF16),
        scratch_shapes=[pltpu.VMEM((1, POOL_HALO + SEQ_CHUNK, POOL_WIDTH), F32)],
        compiler_params=_cparams(("parallel", "parallel"), 32),
        name="pool_prompt",
    )(u, u, wbd, sp)


def _pool_sample_kernel(ext_ref, wbd_ref, sp_ref, o_ref):
    out = _pool_compute(ext_ref, DEC_SEQ, lambda w: float(w), wbd_ref, sp_ref)
    o_ref[...] = out.reshape(SAMPLE_BB, DEC_SEQ, POOL_WIDTH).astype(BF16)


def _pool_sample(ext, wbd, sp):
    return pl.pallas_call(
        _pool_sample_kernel,
        grid=(DEC_BATCH // SAMPLE_BB,),
        in_specs=[
            pl.BlockSpec((SAMPLE_BB, POOL_HALO + DEC_SEQ, POOL_WIDTH), lambda i: (i, 0, 0)),
            pl.BlockSpec((POOL_WIDTH, POOL_WIDTH), lambda i: (0, 0)),
            pl.BlockSpec((1, POOL_WIDTH), lambda i: (0, 0)),
        ],
        out_specs=pl.BlockSpec((SAMPLE_BB, DEC_SEQ, POOL_WIDTH), lambda i: (i, 0, 0)),
        out_shape=jax.ShapeDtypeStruct((DEC_BATCH, DEC_SEQ, POOL_WIDTH), BF16),
        compiler_params=_cparams(("parallel",), 32),
        name="pool_sample",
    )(ext, wbd, sp)


def _conv_compute(ext_ref, row0, n_rows, wdw_ref, bdw_ref, g_ref, b_ref):
    lead = CONV_HALO - CONV_BUF
    acc = None
    for k in range(CONV_K):
        term = ext_ref[:, pl.ds(row0 + lead + k, n_rows), :] * wdw_ref[k:k + 1, :]
        acc = term if acc is None else acc + term
    zc = acc + bdw_ref[...]
    mu = jnp.mean(zc, axis=-1, keepdims=True)
    xc = zc - mu
    y = xc * lax.rsqrt(jnp.mean(xc * xc, axis=-1, keepdims=True) + EPS)
    y = y * g_ref[...] + b_ref[...]
    return y * jax.nn.sigmoid(y)


def _conv_prompt_kernel(halo_ref, z_ref, wdw_ref, bdw_ref, g_ref, b_ref, o_ref, ext_ref):
    i = pl.program_id(1)
    ext_ref[:, :CONV_HALO, :] = jnp.where(i > 0, halo_ref[...], 0.0)
    ext_ref[:, CONV_HALO:, :] = z_ref[...]
    for r0 in range(0, SEQ_CHUNK, CONV_ROWS):
        o_ref[:, r0:r0 + CONV_ROWS, :] = _conv_compute(
            ext_ref, r0, CONV_ROWS, wdw_ref, bdw_ref, g_ref, b_ref).astype(BF16)


def _conv_prompt(z, wdw, bdw, g, b):
    hb = SEQ_CHUNK // CONV_HALO
    vec = pl.BlockSpec((1, CONV_WIDTH), lambda bb, i: (0, 0))
    return pl.pallas_call(
        _conv_prompt_kernel,
        grid=(BATCH, SEQ // SEQ_CHUNK),
        in_specs=[
            pl.BlockSpec((1, CONV_HALO, CONV_WIDTH),
                         lambda bb, i: (bb, jnp.maximum(i * hb - 1, 0), 0)),
            pl.BlockSpec((1, SEQ_CHUNK, CONV_WIDTH), lambda bb, i: (bb, i, 0)),
            pl.BlockSpec((CONV_K, CONV_WIDTH), lambda bb, i: (0, 0)),
            vec, vec, vec,
        ],
        out_specs=pl.BlockSpec((1, SEQ_CHUNK, CONV_WIDTH), lambda bb, i: (bb, i, 0)),
        out_shape=jax.ShapeDtypeStruct((BATCH, SEQ, CONV_WIDTH), BF16),
        scratch_shapes=[pltpu.VMEM((1, CONV_HALO + SEQ_CHUNK, CONV_WIDTH), F32)],
        compiler_params=_cparams(("parallel", "parallel"), 32),
        name="conv_prompt",
    )(z, z, wdw, bdw, g, b)


def _conv_sample_kernel(ext_ref, wdw_ref, bdw_ref, g_ref, b_ref, o_ref):
    o_ref[...] = _conv_compute(ext_ref, 0, DEC_SEQ, wdw_ref, bdw_ref, g_ref, b_ref).astype(BF16)


def _conv_sample(ext, wdw, bdw, g, b):
    vec = pl.BlockSpec((1, CONV_WIDTH), lambda i: (0, 0))
    return pl.pallas_call(
        _conv_sample_kernel,
        grid=(DEC_BATCH // SAMPLE_BB,),
        in_specs=[
            pl.BlockSpec((SAMPLE_BB, CONV_HALO + DEC_SEQ, CONV_WIDTH), lambda i: (i, 0, 0)),
            pl.BlockSpec((CONV_K, CONV_WIDTH), lambda i: (0, 0)),
            vec, vec, vec,
        ],
        out_specs=pl.BlockSpec((SAMPLE_BB, DEC_SEQ, CONV_WIDTH), lambda i: (i, 0, 0)),
        out_shape=jax.ShapeDtypeStruct((DEC_BATCH, DEC_SEQ, CONV_WIDTH), BF16),
        compiler_params=_cparams(("parallel",), 32),
        name="conv_sample",
    )(ext, wdw, bdw, g, b)


def _attn_prompt_kernel(qt_ref, k_ref, vt_ref, o_ref, m_ref, acc_ref):
    qi = pl.program_id(1)
    qt = qt_ref[0, 0]
    m_ref[...] = jnp.full(m_ref.shape, NEG_INF, F32)
    acc_ref[...] = jnp.zeros(acc_ref.shape, F32)

    def block(j, on_diagonal):
        s = _dot(k_ref[0, j], qt)
        if on_diagonal:
            tok = lax.broadcasted_iota(jnp.int32, s.shape, 1) & (TQ - 1)
            key = lax.broadcasted_iota(jnp.int32, s.shape, 0)
            s = jnp.where(key <= tok, s, NEG_INF)
        m_old = m_ref[...]
        m_new = jnp.maximum(m_old, jnp.max(s, axis=0, keepdims=True))
        p = jnp.exp2(s - m_new).astype(BF16)
        acc_ref[...] = jnp.exp2(m_old - m_new) * acc_ref[...] + _dot(vt_ref[0, j], p)
        m_ref[...] = m_new

    def below_diagonal(j, carry):
        block(j, False)
        return carry

    lax.fori_loop(0, qi, below_diagonal, 0)
    block(qi, True)
    acc = acc_ref[...]
    o_ref[0, 0] = (acc[:KV_LORA] / acc[KV_LORA:]).T.astype(BF16)


def _attn_prompt(qt, k, vt):
    nblk = SEQ // TQ
    return pl.pallas_call(
        _attn_prompt_kernel,
        grid=(BATCH, nblk),
        in_specs=[
            pl.BlockSpec((1, 1, Q_HEAD, ATTN_ROWS), lambda b, i: (b, i, 0, 0)),
            pl.BlockSpec((1, nblk, TQ, Q_HEAD), lambda b, i: (b, 0, 0, 0)),
            pl.BlockSpec((1, nblk, Q_HEAD, TQ), lambda b, i: (b, 0, 0, 0)),
        ],
        out_specs=pl.BlockSpec((1, 1, ATTN_ROWS, KV_LORA), lambda b, i: (b, i, 0, 0)),
        out_shape=jax.ShapeDtypeStruct((BATCH, nblk, ATTN_ROWS, KV_LORA), BF16),
        scratch_shapes=[pltpu.VMEM((1, ATTN_ROWS), F32), pltpu.VMEM((Q_HEAD, ATTN_ROWS), F32)],
        compiler_params=_cparams(("parallel", "arbitrary"), 40),
        name="attn_prompt",
    )(qt, k, vt)


def _attn_sample_kernel(layer, pt_ref, q_ref, kcn_ref, krn_ref, ckv_hbm, krt_hbm, o_ref,
                        ck0, ck1, kr0, kr1, semc, semr):
    b = pl.program_id(0)
    n = pl.num_programs(0)
    ckbufs = (ck0, ck1)
    krbufs = (kr0, kr1)

    def issue(seq, slot):
        for p in range(N_PAGES):
            pid = pt_ref[seq * N_PAGES + p]
            pltpu.make_async_copy(ckv_hbm.at[layer, pid], ckbufs[slot].at[p], semc.at[slot]).start()
            pltpu.make_async_copy(krt_hbm.at[layer, pid], krbufs[slot].at[p], semr.at[slot]).start()

    def drain(slot):
        pltpu.make_async_copy(ckv_hbm.at[layer, pl.ds(0, N_PAGES)], ckbufs[slot], semc.at[slot]).wait()
        pltpu.make_async_copy(krt_hbm.at[layer, pl.ds(0, N_PAGES)], krbufs[slot], semr.at[slot]).wait()

    @pl.when(b == 0)
    def _():
        issue(0, 0)

    def update(state, s, kc):
        m, l, acc = state
        m_new = jnp.maximum(m, jnp.max(s, axis=-1, keepdims=True))
        alpha = jnp.exp2(m - m_new)
        p = jnp.exp2(s - m_new)
        return (m_new, alpha * l + jnp.sum(p, axis=-1, keepdims=True),
                alpha * acc + _dot(p.astype(BF16), kc))

    def step(slot):
        drain(slot)
        issue(jnp.minimum(b + 1, n - 1), 1 - slot)
        q = q_ref[0]
        qa = q[:, :KV_LORA]
        qr = q[:, KV_LORA:KV_LORA + QK_ROPE]
        rows = N_HEADS * DEC_SEQ
        state = (jnp.full((rows, 1), NEG_INF, F32), jnp.zeros((rows, 1), F32),
                 jnp.zeros((rows, KV_LORA), F32))
        kc = ckbufs[slot][...].reshape(PAST_LEN, KV_LORA).astype(BF16)
        kr = jnp.concatenate([krbufs[slot][p] for p in range(N_PAGES)], axis=-1).astype(BF16)
        state = update(state, _dot_nt(qa, kc) + _dot(qr, kr), kc)
        kcn = kcn_ref[0]
        s = _dot_nt(qa, kcn) + _dot(qr, krn_ref[0])
        t = lax.broadcasted_iota(jnp.int32, s.shape, 0) & (DEC_SEQ - 1)
        key = lax.broadcasted_iota(jnp.int32, s.shape, 1)
        _, l, acc = update(state, jnp.where(key <= t, s, NEG_INF), kcn)
        o_ref[0] = (acc / l).astype(BF16)

        @pl.when(b == n - 1)
        def _():
            drain(1 - slot)

    for slot in range(2):
        pl.when(b % 2 == slot)(functools.partial(step, slot))


def _attn_sample(layer, page_table, q, kc_new, krt_new, cache_ckv, cache_krope_t):
    page_ck = pltpu.VMEM((N_PAGES, PAGE_SIZE, KV_LORA), F32)
    page_kr = pltpu.VMEM((N_PAGES, QK_ROPE, PAGE_SIZE), F32)
    grid_spec = pltpu.PrefetchScalarGridSpec(
        num_scalar_prefetch=1,
        grid=(DEC_BATCH,),
        in_specs=[pl.BlockSpec((1, N_HEADS * DEC_SEQ, Q_HEAD), lambda b, pt: (b, 0, 0)),
                  pl.BlockSpec((1, LANES, KV_LORA), lambda b, pt: (b, 0, 0)),
                  pl.BlockSpec((1, QK_ROPE, LANES), lambda b, pt: (b, 0, 0)),
                  pl.BlockSpec(memory_space=pl.ANY),
                  pl.BlockSpec(memory_space=pl.ANY)],
        out_specs=pl.BlockSpec((1, N_HEADS * DEC_SEQ, KV_LORA), lambda b, pt: (b, 0, 0)),
        scratch_shapes=[page_ck, page_ck, page_kr, page_kr,
                        pltpu.SemaphoreType.DMA((2,)), pltpu.SemaphoreType.DMA((2,))],
    )
    return pl.pallas_call(
        functools.partial(_attn_sample_kernel, layer),
        grid_spec=grid_spec,
        out_shape=jax.ShapeDtypeStruct((DEC_BATCH, N_HEADS * DEC_SEQ, KV_LORA), BF16),
        compiler_params=_cparams(("arbitrary",), 40),
        name="attn_sample",
    )(page_table, q, kc_new, krt_new, cache_ckv, cache_krope_t)


def _merge_kernel(with_router, op_ref, lat_ref, oc_ref, gate_ref, x_ref, wuv_ref, wbr_ref,
                  wout_ref, gf_ref, wrh_ref, wrl_ref, xo_ref, h_ref, route_ref):
    o_attn = _dot(lat_ref[...], wuv_ref[...]).astype(BF16)
    br_a = _dot(op_ref[...], wbr_ref[:POOL_WIDTH, :])
    br_b = _dot(o_attn, wbr_ref[POOL_WIDTH:POOL_WIDTH + ATTN_WIDTH, :])
    br_c = _dot(oc_ref[...], wbr_ref[POOL_WIDTH + ATTN_WIDTH:, :])
    merged = (gate_ref[:, :D_MODEL].astype(F32) * br_a
              + gate_ref[:, D_MODEL:2 * D_MODEL].astype(F32) * br_b
              + gate_ref[:, 2 * D_MODEL:].astype(F32) * br_c)
    xn = x_ref[...] + _dot(merged.astype(BF16), wout_ref[...])
    xo_ref[...] = xn
    hn = _rms(xn, gf_ref[...])
    if not with_router:
        h_ref[...] = hn.astype(BF16)
        route_ref[...] = jnp.zeros(route_ref.shape, F32)
        return
    for c in range(ROW_TILE[0]):
        h_ref[:, c, :] = hn[:, c * LANES:(c + 1) * LANES]
    hh = hn.astype(BF16)
    hl = (hn - hh.astype(F32)).astype(BF16)
    lg = _dot(hh, wrh_ref[...]) + _dot(hl, wrh_ref[...]) + _dot(hh, wrl_ref[...])
    lane = lax.broadcasted_iota(jnp.int32, lg.shape, 1).astype(F32)
    lg = jnp.where(lane < N_EXPERTS, lg, NEG_INF)
    m1 = jnp.max(lg, axis=-1, keepdims=True)
    i1 = jnp.min(jnp.where(lg == m1, lane, float(LANES)), axis=-1, keepdims=True)
    lg2 = jnp.where(lane == i1, NEG_INF, lg)
    m2 = jnp.max(lg2, axis=-1, keepdims=True)
    i2 = jnp.min(jnp.where(lg2 == m2, lane, float(LANES)), axis=-1, keepdims=True)
    e = jnp.exp(m2 - m1)
    w1 = 1.0 / (1.0 + e)
    w2 = e / (1.0 + e)
    route_ref[...] = jnp.where(lane == 0, i1, jnp.where(lane == 1, i2, jnp.where(
        lane == 2, w1, jnp.where(lane == 3, w2, 0.0))))


def _merge(with_router, o_pool, lat, o_conv, gates, x, wuv, wbr, wout, gf, wrh, wrl):
    tm = TM_TOKEN
    row = lambda i: (i, 0)
    fix = lambda i: (0, 0)
    if with_router:
        h_spec = pl.BlockSpec((tm,) + ROW_TILE, lambda i: (i, 0, 0))
        h_shape = jax.ShapeDtypeStruct((N_TOK,) + ROW_TILE, F32)
    else:
        h_spec = pl.BlockSpec((tm, D_MODEL), row)
        h_shape = jax.ShapeDtypeStruct((N_TOK, D_MODEL), BF16)
    return pl.pallas_call(
        functools.partial(_merge_kernel, with_router),
        grid=(N_TOK // tm,),
        in_specs=[
            pl.BlockSpec((tm, POOL_WIDTH), row),
            pl.BlockSpec((tm, LAT_COLS), row),
            pl.BlockSpec((tm, CONV_WIDTH), row),
            pl.BlockSpec((tm, N_BRANCH * D_MODEL), row),
            pl.BlockSpec((tm, D_MODEL), row),
            pl.BlockSpec((LAT_COLS, ATTN_WIDTH), fix),
            pl.BlockSpec((D_MODEL, D_MODEL), fix),
            pl.BlockSpec((D_MODEL, D_MODEL), fix),
            pl.BlockSpec((1, D_MODEL), fix),
            pl.BlockSpec((D_MODEL, LANES), fix),
            pl.BlockSpec((D_MODEL, LANES), fix),
        ],
        out_specs=[
            pl.BlockSpec((tm, D_MODEL), row),
            h_spec,
            pl.BlockSpec((tm, LANES), row),
        ],
        out_shape=[
            jax.ShapeDtypeStruct((N_TOK, D_MODEL), F32),
            h_shape,
            jax.ShapeDtypeStruct((N_TOK, LANES), F32),
        ],
        compiler_params=_cparams(("parallel",), 48),
        name="merge_router" if with_router else "merge",
    )(o_pool, lat, o_conv, gates, x, wuv, wbr, wout, gf, wrh, wrl)


def _row_chunk_copy(src_ref, dst_ref, sem):
    return pltpu.make_async_copy(src_ref.at[pl.ds(0, DMA_CHUNK)], dst_ref.at[pl.ds(0, DMA_CHUNK)], sem)


def _dispatch_kernel(pos_ref, h_ref, init_ref, xs_ref, sem):
    del init_ref
    pair0 = pl.program_id(0) * DMA_CHUNK

    def issue(r, carry):
        pltpu.make_async_copy(h_ref.at[r], xs_ref.at[pos_ref[pair0 + r]], sem).start()
        return carry

    lax.fori_loop(0, DMA_CHUNK, issue, 0, unroll=8)
    _row_chunk_copy(h_ref, xs_ref, sem).wait()


def _dispatch(pos, h_rows, init):
    token_chunks = N_TOK // DMA_CHUNK
    grid_spec = pltpu.PrefetchScalarGridSpec(
        num_scalar_prefetch=1,
        grid=(N_PAIRS // DMA_CHUNK,),
        in_specs=[pl.BlockSpec((DMA_CHUNK,) + ROW_TILE,
                               lambda i, pos: (i % token_chunks, 0, 0)),
                  pl.BlockSpec(memory_space=pl.ANY)],
        out_specs=pl.BlockSpec(memory_space=pl.ANY),
        scratch_shapes=[pltpu.SemaphoreType.DMA(())],
    )
    return pl.pallas_call(
        _dispatch_kernel,
        grid_spec=grid_spec,
        out_shape=jax.ShapeDtypeStruct((M_SORTED,) + ROW_TILE, F32),
        input_output_aliases={2: 0},
        compiler_params=_cparams(("arbitrary",), 16),
        name="dispatch",
    )(pos, h_rows, init)


def _expert_ffn_kernel(te_ref, na_ref, xs_ref, wg_ref, wu_ref, wd_ref, ys_ref, xb_ref, acc_ref):
    del te_ref
    i = pl.program_id(0)
    f = pl.program_id(1)
    active = i < na_ref[0]

    @pl.when(active & (f == 0))
    def _():
        xb_ref[...] = jnp.concatenate(
            [xs_ref[:, c, :] for c in range(ROW_TILE[0])], axis=-1).astype(BF16)

    @pl.when(active)
    def _():
        x = xb_ref[...]
        a = _dot(x, wg_ref[...])
        b = _dot(x, wu_ref[...])
        y = _dot((a * jax.nn.sigmoid(a) * b).astype(BF16), wd_ref[...])

        @pl.when(f == 0)
        def _():
            acc_ref[...] = y

        @pl.when(f > 0)
        def _():
            acc_ref[...] += y

    @pl.when(f == pl.num_programs(1) - 1)
    def _():
        y = jnp.where(active, acc_ref[...], 0.0)
        for c in range(ROW_TILE[0]):
            ys_ref[:, c, :] = y[:, c * LANES:(c + 1) * LANES]


def _expert_ffn(tile_expert, n_active, xs, wg, wu, wd):
    tm, tf = TM_EXPERT, TF_EXPERT
    nf = D_FF_EXPERT // tf

    def tile(i, na):
        return jnp.minimum(i, na[0] - 1)

    def hidden(i, f, na):
        return jnp.where(i < na[0], f, nf - 1)

    grid_spec = pltpu.PrefetchScalarGridSpec(
        num_scalar_prefetch=2,
        grid=(EXPERT_TILES, nf),
        in_specs=[
            pl.BlockSpec((tm,) + ROW_TILE, lambda i, f, te, na: (tile(i, na), 0, 0)),
            pl.BlockSpec((None, D_MODEL, tf),
                         lambda i, f, te, na: (te[tile(i, na)], 0, hidden(i, f, na))),
            pl.BlockSpec((None, D_MODEL, tf),
                         lambda i, f, te, na: (te[tile(i, na)], 0, hidden(i, f, na))),
            pl.BlockSpec((None, tf, D_MODEL),
                         lambda i, f, te, na: (te[tile(i, na)], hidden(i, f, na), 0)),
        ],
        out_specs=pl.BlockSpec((tm,) + ROW_TILE, lambda i, f, te, na: (i, 0, 0)),
        scratch_shapes=[pltpu.VMEM((tm, D_MODEL), BF16), pltpu.VMEM((tm, D_MODEL), F32)],
    )
    return pl.pallas_call(
        _expert_ffn_kernel,
        grid_spec=grid_spec,
        out_shape=jax.ShapeDtypeStruct((M_SORTED,) + ROW_TILE, F32),
        compiler_params=_cparams(("arbitrary", "arbitrary"), 48),
        name="expert_ffn",
    )(tile_expert, n_active, xs, wg, wu, wd)


def _combine_kernel(pos_ref, ys_ref, x_ref, route_ref, o_ref, buf_ref, sem):
    i = pl.program_id(0)
    n = pl.num_programs(0)
    tmc = DMA_CHUNK // TOP_K

    def issue(tile, slot):
        def body(r, carry):
            for k in range(TOP_K):
                pltpu.make_async_copy(ys_ref.at[pos_ref[k * N_TOK + tile * tmc + r]],
                                      buf_ref.at[slot, k * tmc + r], sem.at[slot]).start()
            return carry

        lax.fori_loop(0, tmc, body, 0, unroll=8)

    @pl.when(i == 0)
    def _():
        issue(0, 0)

    @pl.when(i + 1 < n)
    def _():
        issue(i + 1, (i + 1) % 2)

    slot = i % 2
    _row_chunk_copy(ys_ref, buf_ref.at[slot], sem.at[slot]).wait()
    w1 = route_ref[:, 2:3]
    w2 = route_ref[:, 3:4]
    for c in range(ROW_TILE[0]):
        cols = slice(c * LANES, (c + 1) * LANES)
        o_ref[:, cols] = x_ref[:, cols] + (w1 * buf_ref[slot, :tmc, c, :]
                                            + w2 * buf_ref[slot, tmc:, c, :])


def _combine(pos, ys, x, route):
    tmc = DMA_CHUNK // TOP_K
    grid_spec = pltpu.PrefetchScalarGridSpec(
        num_scalar_prefetch=1,
        grid=(N_TOK // tmc,),
        in_specs=[
            pl.BlockSpec(memory_space=pl.ANY),
            pl.BlockSpec((tmc, D_MODEL), lambda i, pos: (i, 0)),
            pl.BlockSpec((tmc, LANES), lambda i, pos: (i, 0)),
        ],
        out_specs=pl.BlockSpec((tmc, D_MODEL), lambda i, pos: (i, 0)),
        scratch_shapes=[pltpu.VMEM((2, DMA_CHUNK) + ROW_TILE, F32),
                        pltpu.SemaphoreType.DMA((2,))],
    )
    return pl.pallas_call(
        _combine_kernel,
        grid_spec=grid_spec,
        out_shape=jax.ShapeDtypeStruct((N_TOK, D_MODEL), F32),
        compiler_params=_cparams(("arbitrary",), 32),
        name="combine",
    )(pos, ys, x, route)


def _route(route):
    experts = jnp.concatenate([route[:, 0], route[:, 1]]).astype(jnp.int32)
    one_hot = (experts[:, None] == jnp.arange(N_EXPERTS)[None, :]).astype(jnp.int32)
    running = jnp.cumsum(one_hot, axis=0)
    rank = jnp.sum(one_hot * running, axis=1) - 1
    tiles = (running[-1] + TM_EXPERT - 1) // TM_EXPERT
    tile_end = jnp.cumsum(tiles)
    pos = jnp.sum(one_hot * ((tile_end - tiles) * TM_EXPERT)[None, :], axis=1) + rank
    tile_ids = jnp.arange(EXPERT_TILES, dtype=jnp.int32)
    tile_expert = jnp.minimum(jnp.sum(tile_ids[:, None] >= tile_end[None, :], axis=1),
                              N_EXPERTS - 1)
    return pos.astype(jnp.int32), tile_expert.astype(jnp.int32), tile_end[-1:].astype(jnp.int32)


def _ffn_kernel(h_ref, wg_ref, wu_ref, wd_ref, x_ref, o_ref, acc_ref):
    f = pl.program_id(1)
    h = h_ref[...]
    a = _dot(h, wg_ref[...])
    b = _dot(h, wu_ref[...])
    y = _dot((a * jax.nn.sigmoid(a) * b).astype(BF16), wd_ref[...])

    @pl.when(f == 0)
    def _():
        acc_ref[...] = y

    @pl.when(f > 0)
    def _():
        acc_ref[...] += y

    @pl.when(f == pl.num_programs(1) - 1)
    def _():
        o_ref[...] = x_ref[...] + acc_ref[...]


def _ffn(h, wg, wu, wd, x):
    tm, tf = TM_DENSE, TF_DENSE
    row = lambda i, f: (i, 0)
    return pl.pallas_call(
        _ffn_kernel,
        grid=(N_TOK // tm, D_FF // tf),
        in_specs=[
            pl.BlockSpec((tm, D_MODEL), row),
            pl.BlockSpec((D_MODEL, tf), lambda i, f: (0, f)),
            pl.BlockSpec((D_MODEL, tf), lambda i, f: (0, f)),
            pl.BlockSpec((tf, D_MODEL), lambda i, f: (f, 0)),
            pl.BlockSpec((tm, D_MODEL), row),
        ],
        out_specs=pl.BlockSpec((tm, D_MODEL), row),
        out_shape=jax.ShapeDtypeStruct((N_TOK, D_MODEL), F32),
        scratch_shapes=[pltpu.VMEM((tm, D_MODEL), F32)],
        compiler_params=_cparams(("parallel", "arbitrary"), 56),
        name="ffn",
    )(h, wg, wu, wd, x)


def _final_norm_kernel(x_ref, g_ref, o_ref):
    o_ref[...] = _rms(x_ref[...], g_ref[...])


def _final_norm(x, g):
    tm = TM_NORM
    return pl.pallas_call(
        _final_norm_kernel,
        grid=(N_TOK // tm,),
        in_specs=[pl.BlockSpec((tm, D_MODEL), lambda i: (i, 0)),
                  pl.BlockSpec((1, D_MODEL), lambda i: (0, 0))],
        out_specs=pl.BlockSpec((tm, D_MODEL), lambda i: (i, 0)),
        out_shape=jax.ShapeDtypeStruct((N_TOK, D_MODEL), F32),
        compiler_params=_cparams(("parallel",), 32),
        name="final_norm",
    )(x, g)


def _pad_cols(w, width):
    return jnp.pad(w, ((0, 0), (0, width - w.shape[1])))


def _swap_halves(w):
    half = QK_ROPE // 2
    return jnp.concatenate([w[..., half:], w[..., :half]], axis=-1)


def _prep_w_in(w):
    kr = w[:, OFF_KR:OFF_GLU]
    return jnp.concatenate([
        w[:, OFF_POOL:OFF_KR],
        _pad_cols(kr, LANES), _pad_cols(_swap_halves(kr), LANES),
        w[:, OFF_GLU:]], axis=1).astype(BF16)


def _prep_w_uq(w_uq):
    def per_head(part):
        pad = LANES - part.shape[-1]
        return jnp.pad(part, ((0, 0), (0, 0), (0, pad))).reshape(Q_LORA, N_HEADS * LANES).astype(BF16)

    rope = w_uq[..., QK_NOPE:]
    return per_head(w_uq[..., :QK_NOPE]), per_head(rope), per_head(_swap_halves(rope))


def _prep_w_uk(w_uk):
    wt = jnp.transpose(w_uk, (1, 2, 0))
    return jnp.pad(wt, ((0, 0), (0, LANES - QK_NOPE), (0, 0))).astype(BF16)


def _prep_w_uv(w_uv):
    eye = jnp.eye(N_HEADS, dtype=w_uv.dtype)
    wbd = jnp.einsum('rhd,hg->hrgd', w_uv, eye)
    return wbd.reshape(LAT_COLS, ATTN_WIDTH).astype(BF16)


def _prep_w_pool(w_pool):
    n = len(POOL_WINDOWS)
    eye = jnp.eye(n, dtype=w_pool.dtype)
    return jnp.einsum('gcd,gk->gckd', w_pool, eye).reshape(POOL_WIDTH, POOL_WIDTH).astype(BF16)


def _rope_tables():
    inv = ROPE_THETA ** (-jnp.arange(0, QK_ROPE, 2, dtype=F32) / QK_ROPE)
    pos = jnp.concatenate([jnp.tile(jnp.arange(SEQ), BATCH),
                           jnp.tile(PAST_LEN + jnp.arange(DEC_SEQ), DEC_BATCH)])
    ang = pos.astype(F32)[:, None] * inv[None, :]
    cos, sin = jnp.cos(ang), jnp.sin(ang)
    cos_t = _pad_cols(jnp.concatenate([cos, cos], axis=1), LANES)
    sin_t = _pad_cols(jnp.concatenate([-sin, sin], axis=1), LANES)
    return cos_t, sin_t


def _split_hi_lo(w):
    hi = w.astype(BF16)
    lo = (w - hi.astype(F32)).astype(BF16)
    return hi, lo


def kernel(x_prompt, x_sample, cache_ckv, cache_krope, page_table, state_pool, state_conv, g_mix_norm, w_in, b_gate, w_pool, s_pool, g_q_lat, w_uq, g_kv_lat, w_uk, w_uv, w_dw, b_dw, g_conv_ln, b_conv_ln, w_br, w_out, g_ffn_norm, w_d_gate, w_d_up, w_d_down, w_router, w_e_gate, w_e_up, w_e_down, g_final):
    x = jnp.concatenate([x_prompt.reshape(N_PROMPT, D_MODEL),
                         x_sample.reshape(N_SAMPLE, D_MODEL)], axis=0)
    cos_t, sin_t = _rope_tables()
    pt_flat = page_table.reshape(-1)
    cache_krope_t = jnp.swapaxes(cache_krope, 2, 3)

    ckv_p, kr_p, pool_p, conv_p = [], [], [], []
    ckv_s, kr_s, pool_s, conv_s = [], [], [], []
    for l in range(DEPTH):
        u_pool, c_q, kvb, c_kv, k_r, z, gates = _inproj(
            x, g_mix_norm[l][None], _prep_w_in(w_in[l]), b_gate[l][None],
            g_q_lat[l][None], g_kv_lat[l][None], cos_t, sin_t)

        wn, wa, wb = _prep_w_uq(w_uq[l])
        q = _qproj(c_q, wn, wa, wb, _prep_w_uk(w_uk[l]), cos_t, sin_t)

        wbd = _prep_w_pool(w_pool[l])
        sp = s_pool[l][None]
        u_p = u_pool[:N_PROMPT].reshape(BATCH, SEQ, POOL_WIDTH)
        u_s = u_pool[N_PROMPT:].reshape(DEC_BATCH, DEC_SEQ, POOL_WIDTH)
        pool_ext = jnp.concatenate(
            [jnp.zeros((DEC_BATCH, POOL_HALO - POOL_BUF, POOL_WIDTH), F32), state_pool[l], u_s], axis=1)
        o_pool = jnp.concatenate([
            _pool_prompt(u_p, wbd, sp).reshape(N_PROMPT, POOL_WIDTH),
            _pool_sample(pool_ext, wbd, sp).reshape(N_SAMPLE, POOL_WIDTH)], axis=0)

        z_p = z[:N_PROMPT].reshape(BATCH, SEQ, CONV_WIDTH)
        z_s = z[N_PROMPT:].reshape(DEC_BATCH, DEC_SEQ, CONV_WIDTH)
        conv_ext = jnp.concatenate(
            [jnp.zeros((DEC_BATCH, CONV_HALO - CONV_BUF, CONV_WIDTH), F32), state_conv[l], z_s], axis=1)
        conv_args = (w_dw[l], b_dw[l][None], g_conv_ln[l][None], b_conv_ln[l][None])
        o_conv = jnp.concatenate([
            _conv_prompt(z_p, *conv_args).reshape(N_PROMPT, CONV_WIDTH),
            _conv_sample(conv_ext, *conv_args).reshape(N_SAMPLE, CONV_WIDTH)], axis=0)

        nblk = SEQ // TQ
        kv_p = kvb[:N_PROMPT].reshape(BATCH, nblk, TQ, Q_HEAD)
        vt_p = jnp.concatenate([jnp.swapaxes(kv_p[..., :KV_LORA], 2, 3),
                                jnp.ones((BATCH, nblk, Q_HEAD - KV_LORA, TQ), BF16)], axis=2)
        qt_p = q[:N_PROMPT].reshape(BATCH, nblk, TQ, N_HEADS, Q_HEAD)
        qt_p = jnp.transpose(qt_p, (0, 1, 4, 3, 2)).reshape(BATCH, nblk, Q_HEAD, ATTN_ROWS)
        lat_p = _attn_prompt(qt_p, kv_p, vt_p).reshape(BATCH, nblk, N_HEADS, TQ, KV_LORA)
        lat_p = jnp.swapaxes(lat_p, 2, 3)
        q_s = q[N_PROMPT:].reshape(DEC_BATCH, DEC_SEQ, N_HEADS, Q_HEAD)
        q_s = jnp.swapaxes(q_s, 1, 2).reshape(DEC_BATCH, N_HEADS * DEC_SEQ, Q_HEAD)
        kv_s = kvb[N_PROMPT:].reshape(DEC_BATCH, DEC_SEQ, Q_HEAD)
        kc_new = jnp.pad(kv_s[..., :KV_LORA], ((0, 0), (0, LANES - DEC_SEQ), (0, 0)))
        krt_new = jnp.pad(jnp.swapaxes(kv_s[..., KV_LORA:KV_LORA + QK_ROPE], 1, 2),
                          ((0, 0), (0, 0), (0, LANES - DEC_SEQ)))
        lat_s = _attn_sample(l, pt_flat, q_s, kc_new, krt_new, cache_ckv, cache_krope_t)
        lat_s = jnp.swapaxes(lat_s.reshape(DEC_BATCH, N_HEADS, DEC_SEQ, KV_LORA), 1, 2)
        lat = jnp.concatenate([lat_p.reshape(N_PROMPT, LAT_COLS),
                               lat_s.reshape(N_SAMPLE, LAT_COLS)], axis=0)

        is_moe = l % 2 == 1
        wr = _pad_cols(w_router[l // 2], LANES) if is_moe else jnp.zeros((D_MODEL, LANES), F32)
        wrh, wrl = _split_hi_lo(wr)
        x, h, route = _merge(is_moe, o_pool, lat, o_conv, gates, x, _prep_w_uv(w_uv[l]),
                             w_br[l].astype(BF16), w_out[l].astype(BF16), g_ffn_norm[l][None],
                             wrh, wrl)
        if is_moe:
            pos, tile_expert, n_active = _route(route)
            xs = _dispatch(pos, h, jnp.zeros((M_SORTED,) + ROW_TILE, F32))
            ys = _expert_ffn(tile_expert, n_active, xs, w_e_gate[l // 2].astype(BF16),
                             w_e_up[l // 2].astype(BF16), w_e_down[l // 2].astype(BF16))
            x = _combine(pos, ys, x, route)
        else:
            x = _ffn(h, w_d_gate[l // 2].astype(BF16), w_d_up[l // 2].astype(BF16),
                     w_d_down[l // 2].astype(BF16), x)

        ckv_p.append(c_kv[:N_PROMPT].reshape(BATCH, SEQ, KV_LORA))
        kr_p.append(k_r[:N_PROMPT].reshape(BATCH, SEQ, QK_ROPE))
        pool_p.append(u_p[:, -POOL_BUF:])
        conv_p.append(z_p[:, -CONV_BUF:])
        ckv_s.append(c_kv[N_PROMPT:].reshape(DEC_BATCH, DEC_SEQ, KV_LORA))
        kr_s.append(k_r[N_PROMPT:].reshape(DEC_BATCH, DEC_SEQ, QK_ROPE))
        pool_s.append(jnp.concatenate([state_pool[l], u_s], axis=1)[:, -POOL_BUF:])
        conv_s.append(jnp.concatenate([state_conv[l], z_s], axis=1)[:, -CONV_BUF:])

    y = _final_norm(x, g_final[None])
    return (y[:N_PROMPT].reshape(BATCH, SEQ, D_MODEL),
            y[N_PROMPT:].reshape(DEC_BATCH, DEC_SEQ, D_MODEL),
            jnp.stack(ckv_p), jnp.stack(kr_p), jnp.stack(pool_p), jnp.stack(conv_p),
            jnp.stack(ckv_s), jnp.stack(kr_s), jnp.stack(pool_s), jnp.stack(conv_s))
```

```python
import functools

import jax
import jax.numpy as jnp
from jax import lax
from jax.experimental import pallas as pl
from jax.experimental.pallas import tpu as pltpu

F32 = jnp.float32
BF16 = jnp.bfloat16

D_MODEL = 1024
BATCH = 8
SEQ = 2048
DEPTH = 4
DEC_BATCH = 128
DEC_SEQ = 8
PAST_LEN = 8192
PAGE_SIZE = 128
N_PAGES = PAST_LEN // PAGE_SIZE

POOL_WINDOWS = (2, 4, 8, 16)
POOL_GROUP = 64
POOL_WIDTH = 256
POOL_BUF = 15

N_HEADS = 8
QK_NOPE = 64
QK_ROPE = 32
V_DIM = 64
Q_LORA = 256
KV_LORA = 128
ROPE_THETA = 10000.0
ATTN_WIDTH = N_HEADS * V_DIM
ATTN_SCALE = (QK_NOPE + QK_ROPE) ** -0.5

CONV_WIDTH = 256
CONV_K = 31
CONV_BUF = CONV_K - 1

N_BRANCH = 3
OFF_POOL = 0
OFF_Q = OFF_POOL + POOL_WIDTH
OFF_KV = OFF_Q + Q_LORA
OFF_KR = OFF_KV + KV_LORA
OFF_GLU = OFF_KR + QK_ROPE
OFF_GATE = OFF_GLU + 2 * CONV_WIDTH

D_FF = 2816
N_EXPERTS = 8
D_FF_EXPERT = 3584

EPS = 1e-6
NEG_INF = -1e30

N_PROMPT = BATCH * SEQ
N_SAMPLE = DEC_BATCH * DEC_SEQ
N_TOK = N_PROMPT + N_SAMPLE

LANES = 128
VMEM_BYTES_V7X = 64 * 1024 * 1024

C_POOL = 0
C_Q = C_POOL + POOL_WIDTH
C_KV = C_Q + Q_LORA
C_KRA = C_KV + KV_LORA
C_KRB = C_KRA + LANES
C_GLU = C_KRB + LANES
C_GATE = C_GLU + 2 * CONV_WIDTH
C_END = C_GATE + N_BRANCH * D_MODEL

Q_HEAD = 2 * LANES
Q_COLS = N_HEADS * Q_HEAD
LAT_COLS = N_HEADS * KV_LORA

TM_TOKEN = 512
TM_DENSE = 512
TF_DENSE = 1408
TM_EXPERT = 512
TF_EXPERT = 1792
TOP_K = 2
N_PAIRS = TOP_K * N_TOK
EXPERT_TILES = N_PAIRS // TM_EXPERT + N_EXPERTS
M_SORTED = EXPERT_TILES * TM_EXPERT
ROW_TILE = (8, LANES)
DMA_CHUNK = 256
TM_NORM = 1024
TQ = 256
ATTN_ROWS = N_HEADS * TQ
LOG2_E = 1.4426950408889634
Q_SCALE = ATTN_SCALE * LOG2_E
POOL_HALO = 16
CONV_HALO = 32
SEQ_CHUNK = 256
CONV_ROWS = 64
SAMPLE_BB = 16


def _cparams(semantics, vmem_mib):
    assert vmem_mib * 1024 * 1024 < VMEM_BYTES_V7X
    return pltpu.CompilerParams(dimension_semantics=semantics,
                                vmem_limit_bytes=vmem_mib * 1024 * 1024)


def _rms(x, g):
    return x * lax.rsqrt(jnp.mean(x * x, axis=-1, keepdims=True) + EPS) * g


def _dot(a, b):
    return jnp.dot(a, b, preferred_element_type=F32)


def _dot_nt(a, b):
    return lax.dot_general(a, b, (((1,), (1,)), ((), ())), preferred_element_type=F32)


def _inproj_kernel(x_ref, g_ref, w_ref, bg_ref, gq_ref, gkv_ref, cos_ref, sin_ref,
                   up_ref, cq_ref, kvb_ref, ckv_ref, kr_ref, z_ref, gate_ref):
    h = _rms(x_ref[...], g_ref[...]).astype(BF16)

    def proj(lo, hi):
        return _dot(h, w_ref[:, lo:hi])

    up_ref[...] = proj(C_POOL, C_Q)
    cq_ref[...] = _rms(proj(C_Q, C_KV), gq_ref[...]).astype(BF16)
    ckv = _rms(proj(C_KV, C_KRA), gkv_ref[...])
    kr = proj(C_KRA, C_KRB) * cos_ref[...] + proj(C_KRB, C_GLU) * sin_ref[...]
    ckv_ref[...] = ckv
    kr_ref[...] = kr[:, :QK_ROPE]
    kvb_ref[:, :KV_LORA] = ckv.astype(BF16)
    kvb_ref[:, KV_LORA:] = kr.astype(BF16)
    glu = proj(C_GLU, C_GATE)
    z_ref[...] = glu[:, :CONV_WIDTH] * jax.nn.sigmoid(glu[:, CONV_WIDTH:])
    for c in range(N_BRANCH):
        lo = c * D_MODEL
        g = proj(C_GATE + lo, C_GATE + lo + D_MODEL) + bg_ref[:, lo:lo + D_MODEL]
        gate_ref[:, lo:lo + D_MODEL] = jax.nn.sigmoid(g).astype(BF16)


def _inproj(x, g, w, bg, gq, gkv, cos, sin):
    tm = TM_TOKEN
    row = lambda i: (i, 0)
    fix = lambda i: (0, 0)
    return pl.pallas_call(
        _inproj_kernel,
        grid=(N_TOK // tm,),
        in_specs=[
            pl.BlockSpec((tm, D_MODEL), row),
            pl.BlockSpec((1, D_MODEL), fix),
            pl.BlockSpec((D_MODEL, C_END), fix),
            pl.BlockSpec((1, N_BRANCH * D_MODEL), fix),
            pl.BlockSpec((1, Q_LORA), fix),
            pl.BlockSpec((1, KV_LORA), fix),
            pl.BlockSpec((tm, LANES), row),
            pl.BlockSpec((tm, LANES), row),
        ],
        out_specs=[
            pl.BlockSpec((tm, POOL_WIDTH), row),
            pl.BlockSpec((tm, Q_LORA), row),
            pl.BlockSpec((tm, Q_HEAD), row),
            pl.BlockSpec((tm, KV_LORA), row),
            pl.BlockSpec((tm, QK_ROPE), row),
            pl.BlockSpec((tm, CONV_WIDTH), row),
            pl.BlockSpec((tm, N_BRANCH * D_MODEL), row),
        ],
        out_shape=[
            jax.ShapeDtypeStruct((N_TOK, POOL_WIDTH), F32),
            jax.ShapeDtypeStruct((N_TOK, Q_LORA), BF16),
            jax.ShapeDtypeStruct((N_TOK, Q_HEAD), BF16),
            jax.ShapeDtypeStruct((N_TOK, KV_LORA), F32),
            jax.ShapeDtypeStruct((N_TOK, QK_ROPE), F32),
            jax.ShapeDtypeStruct((N_TOK, CONV_WIDTH), F32),
            jax.ShapeDtypeStruct((N_TOK, N_BRANCH * D_MODEL), BF16),
        ],
        compiler_params=_cparams(("parallel",), 56),
        name="inproj",
    )(x, g, w, bg, gq, gkv, cos, sin)


def _q_heads(cq_ref, wn_ref, wa_ref, wb_ref, wuk_ref, cos_ref, sin_ref):
    cq = cq_ref[...]
    qn = _dot(cq, wn_ref[...]).astype(BF16)
    ra = _dot(cq, wa_ref[...])
    rb = _dot(cq, wb_ref[...])
    cos = cos_ref[...]
    sin = sin_ref[...]
    for h in range(N_HEADS):
        sl = slice(h * LANES, (h + 1) * LANES)
        yield (h, _dot(qn[:, sl], wuk_ref[h]) * Q_SCALE,
               (ra[:, sl] * cos + rb[:, sl] * sin) * Q_SCALE)


def _qproj_sample_kernel(*refs):
    q_ref = refs[-1]
    for h, q_abs, q_rope in _q_heads(*refs[:-1]):
        q_ref[:, h * Q_HEAD:h * Q_HEAD + LANES] = q_abs.astype(BF16)
        q_ref[:, h * Q_HEAD + LANES:(h + 1) * Q_HEAD] = q_rope.astype(BF16)


def _qproj_prompt_kernel(*refs):
    qt_ref = refs[-1]
    for h, q_abs, q_rope in _q_heads(*refs[:-1]):
        for c in range(TM_TOKEN // TQ):
            rows = slice(c * TQ, (c + 1) * TQ)
            cols = slice(h * TQ, (h + 1) * TQ)
            qt_ref[0, c, :LANES, cols] = q_abs[rows].T.astype(BF16)
            qt_ref[0, c, LANES:, cols] = q_rope[rows].T.astype(BF16)


def _qproj(for_prompt, cq, wn, wa, wb, wuk, cos, sin):
    tm = TM_TOKEN
    fix = lambda i: (0, 0)
    if for_prompt:
        n_steps, first = N_PROMPT // tm, 0
        tiles = tm // TQ
        per_seq = SEQ // tm
        out_spec = pl.BlockSpec((1, tiles, Q_HEAD, ATTN_ROWS),
                                lambda i: (i // per_seq, i % per_seq, 0, 0))
        out_shape = jax.ShapeDtypeStruct((BATCH, SEQ // TQ, Q_HEAD, ATTN_ROWS), BF16)
    else:
        n_steps, first = N_SAMPLE // tm, N_PROMPT // tm
        out_spec = pl.BlockSpec((tm, Q_COLS), lambda i: (i, 0))
        out_shape = jax.ShapeDtypeStruct((N_SAMPLE, Q_COLS), BF16)
    row = lambda i: (first + i, 0)
    return pl.pallas_call(
        _qproj_prompt_kernel if for_prompt else _qproj_sample_kernel,
        grid=(n_steps,),
        in_specs=[
            pl.BlockSpec((tm, Q_LORA), row),
            pl.BlockSpec((Q_LORA, N_HEADS * LANES), fix),
            pl.BlockSpec((Q_LORA, N_HEADS * LANES), fix),
            pl.BlockSpec((Q_LORA, N_HEADS * LANES), fix),
            pl.BlockSpec((N_HEADS, LANES, KV_LORA), lambda i: (0, 0, 0)),
            pl.BlockSpec((tm, LANES), row),
            pl.BlockSpec((tm, LANES), row),
        ],
        out_specs=out_spec,
        out_shape=out_shape,
        compiler_params=_cparams(("parallel",), 40),
        name="qproj_prompt" if for_prompt else "qproj_sample",
    )(cq, wn, wa, wb, wuk, cos, sin)


def _pool_compute(ext_ref, n_rows, cnt_of_window, wbd_ref, sp_ref):
    def ld(j):
        return ext_ref[:, pl.ds(POOL_HALO - j, n_rows), :]

    tok = ld(0)
    run = tok
    sums = {}
    for j in range(1, max(POOL_WINDOWS)):
        run = run + ld(j)
        if j + 1 in POOL_WINDOWS:
            sums[j + 1] = run
    lane = lax.broadcasted_iota(jnp.int32, tok.shape, 2)
    pooled = sums[POOL_WINDOWS[-1]] / cnt_of_window(POOL_WINDOWS[-1])
    for g in range(len(POOL_WINDOWS) - 2, -1, -1):
        w = POOL_WINDOWS[g]
        pooled = jnp.where(lane < (g + 1) * POOL_GROUP, sums[w] / cnt_of_window(w), pooled)
    pooled = (pooled - tok).reshape(-1, POOL_WIDTH).astype(BF16)
    return _dot(pooled, wbd_ref[...]) * sp_ref[...]


def _pool_prompt_kernel(halo_ref, u_ref, wbd_ref, sp_ref, o_ref, ext_ref):
    i = pl.program_id(1)
    ext_ref[:, :POOL_HALO, :] = jnp.where(i > 0, halo_ref[...], 0.0)
    ext_ref[:, POOL_HALO:, :] = u_ref[...]
    pos = lax.broadcasted_iota(jnp.int32, (1, SEQ_CHUNK, 1), 1) + i * SEQ_CHUNK

    def cnt(w):
        return jnp.minimum(pos + 1, w).astype(F32)

    o_ref[0] = _pool_compute(ext_ref, SEQ_CHUNK, cnt, wbd_ref, sp_ref).astype(BF16)


def _pool_prompt(u, wbd, sp):
    hb = SEQ_CHUNK // POOL_HALO
    return pl.pallas_call(
        _pool_prompt_kernel,
        grid=(BATCH, SEQ // SEQ_CHUNK),
        in_specs=[
            pl.BlockSpec((1, POOL_HALO, POOL_WIDTH),
                         lambda b, i: (b, jnp.maximum(i * hb - 1, 0), 0)),
            pl.BlockSpec((1, SEQ_CHUNK, POOL_WIDTH), lambda b, i: (b, i, 0)),
            pl.BlockSpec((POOL_WIDTH, POOL_WIDTH), lambda b, i: (0, 0)),
            pl.BlockSpec((1, POOL_WIDTH), lambda b, i: (0, 0)),
        ],
        out_specs=pl.BlockSpec((1, SEQ_CHUNK, POOL_WIDTH), lambda b, i: (b, i, 0)),
        out_shape=jax.ShapeDtypeStruct((BATCH, SEQ, POOL_WIDTH), BF16),
        scratch_shapes=[pltpu.VMEM((1, POOL_HALO + SEQ_CHUNK, POOL_WIDTH), F32)],
        compiler_params=_cparams(("parallel", "parallel"), 32),
        name="pool_prompt",
    )(u, u, wbd, sp)


def _pool_sample_kernel(ext_ref, wbd_ref, sp_ref, o_ref):
    out = _pool_compute(ext_ref, DEC_SEQ, lambda w: float(w), wbd_ref, sp_ref)
    o_ref[...] = out.reshape(SAMPLE_BB, DEC_SEQ, POOL_WIDTH).astype(BF16)


def _pool_sample(ext, wbd, sp):
    return pl.pallas_call(
        _pool_sample_kernel,
        grid=(DEC_BATCH // SAMPLE_BB,),
        in_specs=[
            pl.BlockSpec((SAMPLE_BB, POOL_HALO + DEC_SEQ, POOL_WIDTH), lambda i: (i, 0, 0)),
            pl.BlockSpec((POOL_WIDTH, POOL_WIDTH), lambda i: (0, 0)),
            pl.BlockSpec((1, POOL_WIDTH), lambda i: (0, 0)),
        ],
        out_specs=pl.BlockSpec((SAMPLE_BB, DEC_SEQ, POOL_WIDTH), lambda i: (i, 0, 0)),
        out_shape=jax.ShapeDtypeStruct((DEC_BATCH, DEC_SEQ, POOL_WIDTH), BF16),
        compiler_params=_cparams(("parallel",), 32),
        name="pool_sample",
    )(ext, wbd, sp)


def _conv_compute(ext_ref, row0, n_rows, wdw_ref, bdw_ref, g_ref, b_ref):
    lead = CONV_HALO - CONV_BUF
    acc = None
    for k in range(CONV_K):
        term = ext_ref[:, pl.ds(row0 + lead + k, n_rows), :] * wdw_ref[k:k + 1, :]
        acc = term if acc is None else acc + term
    zc = acc + bdw_ref[...]
    mu = jnp.mean(zc, axis=-1, keepdims=True)
    xc = zc - mu
    y = xc * lax.rsqrt(jnp.mean(xc * xc, axis=-1, keepdims=True) + EPS)
    y = y * g_ref[...] + b_ref[...]
    return y * jax.nn.sigmoid(y)


def _conv_prompt_kernel(halo_ref, z_ref, wdw_ref, bdw_ref, g_ref, b_ref, o_ref, ext_ref):
    i = pl.program_id(1)
    ext_ref[:, :CONV_HALO, :] = jnp.where(i > 0, halo_ref[...], 0.0)
    ext_ref[:, CONV_HALO:, :] = z_ref[...]
    for r0 in range(0, SEQ_CHUNK, CONV_ROWS):
        o_ref[:, r0:r0 + CONV_ROWS, :] = _conv_compute(
            ext_ref, r0, CONV_ROWS, wdw_ref, bdw_ref, g_ref, b_ref).astype(BF16)


def _conv_prompt(z, wdw, bdw, g, b):
    hb = SEQ_CHUNK // CONV_HALO
    vec = pl.BlockSpec((1, CONV_WIDTH), lambda bb, i: (0, 0))
    return pl.pallas_call(
        _conv_prompt_kernel,
        grid=(BATCH, SEQ // SEQ_CHUNK),
        in_specs=[
            pl.BlockSpec((1, CONV_HALO, CONV_WIDTH),
                         lambda bb, i: (bb, jnp.maximum(i * hb - 1, 0), 0)),
            pl.BlockSpec((1, SEQ_CHUNK, CONV_WIDTH), lambda bb, i: (bb, i, 0)),
            pl.BlockSpec((CONV_K, CONV_WIDTH), lambda bb, i: (0, 0)),
            vec, vec, vec,
        ],
        out_specs=pl.BlockSpec((1, SEQ_CHUNK, CONV_WIDTH), lambda bb, i: (bb, i, 0)),
        out_shape=jax.ShapeDtypeStruct((BATCH, SEQ, CONV_WIDTH), BF16),
        scratch_shapes=[pltpu.VMEM((1, CONV_HALO + SEQ_CHUNK, CONV_WIDTH), F32)],
        compiler_params=_cparams(("parallel", "parallel"), 32),
        name="conv_prompt",
    )(z, z, wdw, bdw, g, b)


def _conv_sample_kernel(ext_ref, wdw_ref, bdw_ref, g_ref, b_ref, o_ref):
    o_ref[...] = _conv_compute(ext_ref, 0, DEC_SEQ, wdw_ref, bdw_ref, g_ref, b_ref).astype(BF16)


def _conv_sample(ext, wdw, bdw, g, b):
    vec = pl.BlockSpec((1, CONV_WIDTH), lambda i: (0, 0))
    return pl.pallas_call(
        _conv_sample_kernel,
        grid=(DEC_BATCH // SAMPLE_BB,),
        in_specs=[
            pl.BlockSpec((SAMPLE_BB, CONV_HALO + DEC_SEQ, CONV_WIDTH), lambda i: (i, 0, 0)),
            pl.BlockSpec((CONV_K, CONV_WIDTH), lambda i: (0, 0)),
            vec, vec, vec,
        ],
        out_specs=pl.BlockSpec((SAMPLE_BB, DEC_SEQ, CONV_WIDTH), lambda i: (i, 0, 0)),
        out_shape=jax.ShapeDtypeStruct((DEC_BATCH, DEC_SEQ, CONV_WIDTH), BF16),
        compiler_params=_cparams(("parallel",), 32),
        name="conv_sample",
    )(ext, wdw, bdw, g, b)


def _attn_prompt_kernel(qt_ref, k_ref, vt_ref, o_ref, m_ref, acc_ref):
    qi = pl.program_id(1)
    qt = qt_ref[0, 0]
    m_ref[...] = jnp.full(m_ref.shape, NEG_INF, F32)
    acc_ref[...] = jnp.zeros(acc_ref.shape, F32)

    def block(j, on_diagonal):
        s = _dot(k_ref[0, j], qt)
        if on_diagonal:
            tok = lax.broadcasted_iota(jnp.int32, s.shape, 1) & (TQ - 1)
            key = lax.broadcasted_iota(jnp.int32, s.shape, 0)
            s = jnp.where(key <= tok, s, NEG_INF)
        m_old = m_ref[...]
        m_new = jnp.maximum(m_old, jnp.max(s, axis=0, keepdims=True))
        p = jnp.exp2(s - m_new).astype(BF16)
        acc_ref[...] = jnp.exp2(m_old - m_new) * acc_ref[...] + _dot(vt_ref[0, j], p)
        m_ref[...] = m_new

    def below_diagonal(j, carry):
        block(j, False)
        return carry

    lax.fori_loop(0, qi, below_diagonal, 0)
    block(qi, True)
    acc = acc_ref[...]
    out = (acc[:KV_LORA] / acc[KV_LORA:]).T.astype(BF16)
    for h in range(N_HEADS):
        o_ref[0, :, h * KV_LORA:(h + 1) * KV_LORA] = out[h * TQ:(h + 1) * TQ]


def _attn_prompt(qt, k, vt):
    nblk = SEQ // TQ
    return pl.pallas_call(
        _attn_prompt_kernel,
        grid=(BATCH, nblk),
        in_specs=[
            pl.BlockSpec((1, 1, Q_HEAD, ATTN_ROWS), lambda b, i: (b, i, 0, 0)),
            pl.BlockSpec((1, nblk, TQ, Q_HEAD), lambda b, i: (b, 0, 0, 0)),
            pl.BlockSpec((1, nblk, Q_HEAD, TQ), lambda b, i: (b, 0, 0, 0)),
        ],
        out_specs=pl.BlockSpec((1, TQ, LAT_COLS), lambda b, i: (b, i, 0)),
        out_shape=jax.ShapeDtypeStruct((BATCH, SEQ, LAT_COLS), BF16),
        scratch_shapes=[pltpu.VMEM((1, ATTN_ROWS), F32), pltpu.VMEM((Q_HEAD, ATTN_ROWS), F32)],
        compiler_params=_cparams(("parallel", "arbitrary"), 40),
        name="attn_prompt",
    )(qt, k, vt)


def _attn_sample_kernel(layer, pt_ref, q_ref, kcn_ref, krn_ref, ckv_hbm, krt_hbm, o_ref,
                        ck0, ck1, kr0, kr1, semc, semr):
    b = pl.program_id(0)
    n = pl.num_programs(0)
    ckbufs = (ck0, ck1)
    krbufs = (kr0, kr1)

    def issue(seq, slot):
        for p in range(N_PAGES):
            pid = pt_ref[seq * N_PAGES + p]
            pltpu.make_async_copy(ckv_hbm.at[layer, pid], ckbufs[slot].at[p],
                                  semc.at[slot]).start(priority=p % 2)
            pltpu.make_async_copy(krt_hbm.at[layer, pid], krbufs[slot].at[p],
                                  semr.at[slot]).start(priority=p % 2)

    def drain(slot):
        pltpu.make_async_copy(ckv_hbm.at[layer, pl.ds(0, N_PAGES)], ckbufs[slot], semc.at[slot]).wait()
        pltpu.make_async_copy(krt_hbm.at[layer, pl.ds(0, N_PAGES)], krbufs[slot], semr.at[slot]).wait()

    @pl.when(b == 0)
    def _():
        issue(0, 0)

    def update(state, s, kc):
        m, l, acc = state
        m_new = jnp.maximum(m, jnp.max(s, axis=-1, keepdims=True))
        alpha = jnp.exp2(m - m_new)
        p = jnp.exp2(s - m_new)
        return (m_new, alpha * l + jnp.sum(p, axis=-1, keepdims=True),
                alpha * acc + _dot(p.astype(BF16), kc))

    def step(slot):
        drain(slot)
        issue(jnp.minimum(b + 1, n - 1), 1 - slot)
        q = q_ref[0]
        qa = q[:, :KV_LORA]
        qr = q[:, KV_LORA:KV_LORA + QK_ROPE]
        rows = N_HEADS * DEC_SEQ
        state = (jnp.full((rows, 1), NEG_INF, F32), jnp.zeros((rows, 1), F32),
                 jnp.zeros((rows, KV_LORA), F32))
        kc = ckbufs[slot][...].reshape(PAST_LEN, KV_LORA).astype(BF16)
        kr = jnp.concatenate([krbufs[slot][p] for p in range(N_PAGES)], axis=-1).astype(BF16)
        state = update(state, _dot_nt(qa, kc) + _dot(qr, kr), kc)
        kcn = kcn_ref[0]
        s = _dot_nt(qa, kcn) + _dot(qr, krn_ref[0])
        t = lax.broadcasted_iota(jnp.int32, s.shape, 0) & (DEC_SEQ - 1)
        key = lax.broadcasted_iota(jnp.int32, s.shape, 1)
        _, l, acc = update(state, jnp.where(key <= t, s, NEG_INF), kcn)
        o_ref[0] = (acc / l).astype(BF16)

        @pl.when(b == n - 1)
        def _():
            drain(1 - slot)

    for slot in range(2):
        pl.when(b % 2 == slot)(functools.partial(step, slot))


def _attn_sample(layer, page_table, q, kc_new, krt_new, cache_ckv, cache_krope_t):
    page_ck = pltpu.VMEM((N_PAGES, PAGE_SIZE, KV_LORA), F32)
    page_kr = pltpu.VMEM((N_PAGES, QK_ROPE, PAGE_SIZE), F32)
    grid_spec = pltpu.PrefetchScalarGridSpec(
        num_scalar_prefetch=1,
        grid=(DEC_BATCH,),
        in_specs=[pl.BlockSpec((1, N_HEADS * DEC_SEQ, Q_HEAD), lambda b, pt: (b, 0, 0)),
                  pl.BlockSpec((1, LANES, KV_LORA), lambda b, pt: (b, 0, 0)),
                  pl.BlockSpec((1, QK_ROPE, LANES), lambda b, pt: (b, 0, 0)),
                  pl.BlockSpec(memory_space=pl.ANY),
                  pl.BlockSpec(memory_space=pl.ANY)],
        out_specs=pl.BlockSpec((1, N_HEADS * DEC_SEQ, KV_LORA), lambda b, pt: (b, 0, 0)),
        scratch_shapes=[page_ck, page_ck, page_kr, page_kr,
                        pltpu.SemaphoreType.DMA((2,)), pltpu.SemaphoreType.DMA((2,))],
    )
    return pl.pallas_call(
        functools.partial(_attn_sample_kernel, layer),
        grid_spec=grid_spec,
        out_shape=jax.ShapeDtypeStruct((DEC_BATCH, N_HEADS * DEC_SEQ, KV_LORA), BF16),
        compiler_params=_cparams(("arbitrary",), 40),
        name="attn_sample",
    )(page_table, q, kc_new, krt_new, cache_ckv, cache_krope_t)


def _merge_kernel(with_router, op_p, op_s, lat_p, lat_s, oc_p, oc_s, gate_ref, x_ref, wuv_ref,
                  wbr_ref, wout_ref, gf_ref, wrh_ref, wrl_ref, xo_ref, h_ref, route_ref):
    in_prompt = pl.program_id(0) < N_PROMPT // TM_TOKEN

    def pick(prompt_ref, sample_ref):
        return jnp.where(in_prompt, prompt_ref[...], sample_ref[...])

    o_attn = _dot(pick(lat_p, lat_s), wuv_ref[...]).astype(BF16)
    br_a = _dot(pick(op_p, op_s), wbr_ref[:POOL_WIDTH, :])
    br_b = _dot(o_attn, wbr_ref[POOL_WIDTH:POOL_WIDTH + ATTN_WIDTH, :])
    br_c = _dot(pick(oc_p, oc_s), wbr_ref[POOL_WIDTH + ATTN_WIDTH:, :])
    merged = (gate_ref[:, :D_MODEL].astype(F32) * br_a
              + gate_ref[:, D_MODEL:2 * D_MODEL].astype(F32) * br_b
              + gate_ref[:, 2 * D_MODEL:].astype(F32) * br_c)
    xn = x_ref[...] + _dot(merged.astype(BF16), wout_ref[...])
    xo_ref[...] = xn
    hn = _rms(xn, gf_ref[...])
    if not with_router:
        h_ref[...] = hn.astype(BF16)
        route_ref[...] = jnp.zeros(route_ref.shape, F32)
        return
    for c in range(ROW_TILE[0]):
        h_ref[:, c, :] = hn[:, c * LANES:(c + 1) * LANES]
    hh = hn.astype(BF16)
    hl = (hn - hh.astype(F32)).astype(BF16)
    lg = _dot(hh, wrh_ref[...]) + _dot(hl, wrh_ref[...]) + _dot(hh, wrl_ref[...])
    lane = lax.broadcasted_iota(jnp.int32, lg.shape, 1).astype(F32)
    lg = jnp.where(lane < N_EXPERTS, lg, NEG_INF)
    m1 = jnp.max(lg, axis=-1, keepdims=True)
    i1 = jnp.min(jnp.where(lg == m1, lane, float(LANES)), axis=-1, keepdims=True)
    lg2 = jnp.where(lane == i1, NEG_INF, lg)
    m2 = jnp.max(lg2, axis=-1, keepdims=True)
    i2 = jnp.min(jnp.where(lg2 == m2, lane, float(LANES)), axis=-1, keepdims=True)
    e = jnp.exp(m2 - m1)
    w1 = 1.0 / (1.0 + e)
    w2 = e / (1.0 + e)
    route_ref[...] = jnp.where(lane == 0, i1, jnp.where(lane == 1, i2, jnp.where(
        lane == 2, w1, jnp.where(lane == 3, w2, 0.0))))


def _merge(with_router, o_pool, lat, o_conv, gates, x, wuv, wbr, wout, gf, wrh, wrl):
    tm = TM_TOKEN
    row = lambda i: (i, 0)
    fix = lambda i: (0, 0)
    prompt_tiles = N_PROMPT // tm
    prompt_row = lambda i: (jnp.minimum(i, prompt_tiles - 1), 0)
    sample_row = lambda i: (jnp.maximum(i - prompt_tiles, 0), 0)

    def pair(width):
        return [pl.BlockSpec((tm, width), prompt_row), pl.BlockSpec((tm, width), sample_row)]
    if with_router:
        h_spec = pl.BlockSpec((tm,) + ROW_TILE, lambda i: (i, 0, 0))
        h_shape = jax.ShapeDtypeStruct((N_TOK,) + ROW_TILE, F32)
    else:
        h_spec = pl.BlockSpec((tm, D_MODEL), row)
        h_shape = jax.ShapeDtypeStruct((N_TOK, D_MODEL), BF16)
    return pl.pallas_call(
        functools.partial(_merge_kernel, with_router),
        grid=(N_TOK // tm,),
        in_specs=pair(POOL_WIDTH) + pair(LAT_COLS) + pair(CONV_WIDTH) + [
            pl.BlockSpec((tm, N_BRANCH * D_MODEL), row),
            pl.BlockSpec((tm, D_MODEL), row),
            pl.BlockSpec((LAT_COLS, ATTN_WIDTH), fix),
            pl.BlockSpec((D_MODEL, D_MODEL), fix),
            pl.BlockSpec((D_MODEL, D_MODEL), fix),
            pl.BlockSpec((1, D_MODEL), fix),
            pl.BlockSpec((D_MODEL, LANES), fix),
            pl.BlockSpec((D_MODEL, LANES), fix),
        ],
        out_specs=[
            pl.BlockSpec((tm, D_MODEL), row),
            h_spec,
            pl.BlockSpec((tm, LANES), row),
        ],
        out_shape=[
            jax.ShapeDtypeStruct((N_TOK, D_MODEL), F32),
            h_shape,
            jax.ShapeDtypeStruct((N_TOK, LANES), F32),
        ],
        compiler_params=_cparams(("parallel",), 48),
        name="merge_router" if with_router else "merge",
    )(*o_pool, *lat, *o_conv, gates, x, wuv, wbr, wout, gf, wrh, wrl)


def _row_chunk_copy(src_ref, dst_ref, sem):
    return pltpu.make_async_copy(src_ref.at[pl.ds(0, DMA_CHUNK)], dst_ref.at[pl.ds(0, DMA_CHUNK)], sem)


def _dispatch_kernel(pos_ref, h_ref, init_ref, xs_ref, sem):
    del init_ref
    pair0 = pl.program_id(0) * DMA_CHUNK

    def issue(r, carry):
        pltpu.make_async_copy(h_ref.at[r], xs_ref.at[pos_ref[pair0 + r]], sem).start()
        return carry

    lax.fori_loop(0, DMA_CHUNK, issue, 0, unroll=8)
    _row_chunk_copy(h_ref, xs_ref, sem).wait()


def _dispatch(pos, h_rows, init):
    token_chunks = N_TOK // DMA_CHUNK
    grid_spec = pltpu.PrefetchScalarGridSpec(
        num_scalar_prefetch=1,
        grid=(N_PAIRS // DMA_CHUNK,),
        in_specs=[pl.BlockSpec((DMA_CHUNK,) + ROW_TILE,
                               lambda i, pos: (i % token_chunks, 0, 0)),
                  pl.BlockSpec(memory_space=pl.ANY)],
        out_specs=pl.BlockSpec(memory_space=pl.ANY),
        scratch_shapes=[pltpu.SemaphoreType.DMA(())],
    )
    return pl.pallas_call(
        _dispatch_kernel,
        grid_spec=grid_spec,
        out_shape=jax.ShapeDtypeStruct((M_SORTED,) + ROW_TILE, F32),
        input_output_aliases={2: 0},
        compiler_params=_cparams(("arbitrary",), 16),
        name="dispatch",
    )(pos, h_rows, init)


def _expert_ffn_kernel(te_ref, na_ref, xs_ref, wg_ref, wu_ref, wd_ref, ys_ref, xb_ref, acc_ref):
    del te_ref
    i = pl.program_id(0)
    f = pl.program_id(1)
    active = i < na_ref[0]

    @pl.when(active & (f == 0))
    def _():
        xb_ref[...] = jnp.concatenate(
            [xs_ref[:, c, :] for c in range(ROW_TILE[0])], axis=-1).astype(BF16)

    @pl.when(active)
    def _():
        x = xb_ref[...]
        a = _dot(x, wg_ref[...])
        b = _dot(x, wu_ref[...])
        y = _dot((a * jax.nn.sigmoid(a) * b).astype(BF16), wd_ref[...])

        @pl.when(f == 0)
        def _():
            acc_ref[...] = y

        @pl.when(f > 0)
        def _():
            acc_ref[...] += y

    @pl.when(f == pl.num_programs(1) - 1)
    def _():
        y = jnp.where(active, acc_ref[...], 0.0)
        for c in range(ROW_TILE[0]):
            ys_ref[:, c, :] = y[:, c * LANES:(c + 1) * LANES]


def _expert_ffn(moe_layer, tile_expert, n_active, xs, wg, wu, wd):
    tm, tf = TM_EXPERT, TF_EXPERT
    nf = D_FF_EXPERT // tf

    def tile(i, na):
        return jnp.minimum(i, na[0] - 1)

    def hidden(i, f, na):
        return jnp.where(i < na[0], f, nf - 1)

    grid_spec = pltpu.PrefetchScalarGridSpec(
        num_scalar_prefetch=2,
        grid=(EXPERT_TILES, nf),
        in_specs=[
            pl.BlockSpec((tm,) + ROW_TILE, lambda i, f, te, na: (tile(i, na), 0, 0)),
            pl.BlockSpec((None, None, D_MODEL, tf),
                         lambda i, f, te, na: (moe_layer, te[tile(i, na)], 0, hidden(i, f, na))),
            pl.BlockSpec((None, None, D_MODEL, tf),
                         lambda i, f, te, na: (moe_layer, te[tile(i, na)], 0, hidden(i, f, na))),
            pl.BlockSpec((None, None, tf, D_MODEL),
                         lambda i, f, te, na: (moe_layer, te[tile(i, na)], hidden(i, f, na), 0)),
        ],
        out_specs=pl.BlockSpec((tm,) + ROW_TILE, lambda i, f, te, na: (i, 0, 0)),
        scratch_shapes=[pltpu.VMEM((tm, D_MODEL), BF16), pltpu.VMEM((tm, D_MODEL), F32)],
    )
    return pl.pallas_call(
        _expert_ffn_kernel,
        grid_spec=grid_spec,
        out_shape=jax.ShapeDtypeStruct((M_SORTED,) + ROW_TILE, F32),
        compiler_params=_cparams(("arbitrary", "arbitrary"), 56),
        name="expert_ffn",
    )(tile_expert, n_active, xs, wg, wu, wd)


def _combine_kernel(pos_ref, ys_ref, x_ref, route_ref, o_ref, buf_ref, sem):
    i = pl.program_id(0)
    n = pl.num_programs(0)
    tmc = DMA_CHUNK // TOP_K

    def issue(tile, slot):
        def body(r, carry):
            for k in range(TOP_K):
                pltpu.make_async_copy(ys_ref.at[pos_ref[k * N_TOK + tile * tmc + r]],
                                      buf_ref.at[slot, k * tmc + r], sem.at[slot]).start()
            return carry

        lax.fori_loop(0, tmc, body, 0, unroll=8)

    @pl.when(i == 0)
    def _():
        issue(0, 0)

    @pl.when(i + 1 < n)
    def _():
        issue(i + 1, (i + 1) % 2)

    slot = i % 2
    _row_chunk_copy(ys_ref, buf_ref.at[slot], sem.at[slot]).wait()
    w1 = route_ref[:, 2:3]
    w2 = route_ref[:, 3:4]
    for c in range(ROW_TILE[0]):
        cols = slice(c * LANES, (c + 1) * LANES)
        o_ref[:, cols] = x_ref[:, cols] + (w1 * buf_ref[slot, :tmc, c, :]
                                            + w2 * buf_ref[slot, tmc:, c, :])


def _combine(pos, ys, x, route):
    tmc = DMA_CHUNK // TOP_K
    grid_spec = pltpu.PrefetchScalarGridSpec(
        num_scalar_prefetch=1,
        grid=(N_TOK // tmc,),
        in_specs=[
            pl.BlockSpec(memory_space=pl.ANY),
            pl.BlockSpec((tmc, D_MODEL), lambda i, pos: (i, 0)),
            pl.BlockSpec((tmc, LANES), lambda i, pos: (i, 0)),
        ],
        out_specs=pl.BlockSpec((tmc, D_MODEL), lambda i, pos: (i, 0)),
        scratch_shapes=[pltpu.VMEM((2, DMA_CHUNK) + ROW_TILE, F32),
                        pltpu.SemaphoreType.DMA((2,))],
    )
    return pl.pallas_call(
        _combine_kernel,
        grid_spec=grid_spec,
        out_shape=jax.ShapeDtypeStruct((N_TOK, D_MODEL), F32),
        compiler_params=_cparams(("arbitrary",), 32),
        name="combine",
    )(pos, ys, x, route)


def _route(route):
    experts = jnp.concatenate([route[:, 0], route[:, 1]]).astype(jnp.int32)
    one_hot = (experts[:, None] == jnp.arange(N_EXPERTS)[None, :]).astype(jnp.int32)
    running = jnp.cumsum(one_hot, axis=0)
    rank = jnp.sum(one_hot * running, axis=1) - 1
    tiles = (running[-1] + TM_EXPERT - 1) // TM_EXPERT
    tile_end = jnp.cumsum(tiles)
    pos = jnp.sum(one_hot * ((tile_end - tiles) * TM_EXPERT)[None, :], axis=1) + rank
    tile_ids = jnp.arange(EXPERT_TILES, dtype=jnp.int32)
    tile_expert = jnp.minimum(jnp.sum(tile_ids[:, None] >= tile_end[None, :], axis=1),
                              N_EXPERTS - 1)
    return pos.astype(jnp.int32), tile_expert.astype(jnp.int32), tile_end[-1:].astype(jnp.int32)


def _ffn_kernel(h_ref, wg_ref, wu_ref, wd_ref, x_ref, o_ref, acc_ref):
    f = pl.program_id(1)
    h = h_ref[...]
    a = _dot(h, wg_ref[...])
    b = _dot(h, wu_ref[...])
    y = _dot((a * jax.nn.sigmoid(a) * b).astype(BF16), wd_ref[...])

    @pl.when(f == 0)
    def _():
        acc_ref[...] = y

    @pl.when(f > 0)
    def _():
        acc_ref[...] += y

    @pl.when(f == pl.num_programs(1) - 1)
    def _():
        o_ref[...] = x_ref[...] + acc_ref[...]


def _ffn(h, wg, wu, wd, x):
    tm, tf = TM_DENSE, TF_DENSE
    row = lambda i, f: (i, 0)
    return pl.pallas_call(
        _ffn_kernel,
        grid=(N_TOK // tm, D_FF // tf),
        in_specs=[
            pl.BlockSpec((tm, D_MODEL), row),
            pl.BlockSpec((D_MODEL, tf), lambda i, f: (0, f)),
            pl.BlockSpec((D_MODEL, tf), lambda i, f: (0, f)),
            pl.BlockSpec((tf, D_MODEL), lambda i, f: (f, 0)),
            pl.BlockSpec((tm, D_MODEL), row),
        ],
        out_specs=pl.BlockSpec((tm, D_MODEL), row),
        out_shape=jax.ShapeDtypeStruct((N_TOK, D_MODEL), F32),
        scratch_shapes=[pltpu.VMEM((tm, D_MODEL), F32)],
        compiler_params=_cparams(("parallel", "arbitrary"), 56),
        name="ffn",
    )(h, wg, wu, wd, x)


def _final_norm_kernel(x_ref, g_ref, o_ref):
    o_ref[...] = _rms(x_ref[...], g_ref[...])


def _final_norm(x, g):
    tm = TM_NORM
    return pl.pallas_call(
        _final_norm_kernel,
        grid=(N_TOK // tm,),
        in_specs=[pl.BlockSpec((tm, D_MODEL), lambda i: (i, 0)),
                  pl.BlockSpec((1, D_MODEL), lambda i: (0, 0))],
        out_specs=pl.BlockSpec((tm, D_MODEL), lambda i: (i, 0)),
        out_shape=jax.ShapeDtypeStruct((N_TOK, D_MODEL), F32),
        compiler_params=_cparams(("parallel",), 32),
        name="final_norm",
    )(x, g)


def _pad_cols(w, width):
    return jnp.pad(w, ((0, 0), (0, width - w.shape[1])))


def _swap_halves(w):
    half = QK_ROPE // 2
    return jnp.concatenate([w[..., half:], w[..., :half]], axis=-1)


def _prep_w_in(w):
    kr = w[:, OFF_KR:OFF_GLU]
    return jnp.concatenate([
        w[:, OFF_POOL:OFF_KR],
        _pad_cols(kr, LANES), _pad_cols(_swap_halves(kr), LANES),
        w[:, OFF_GLU:]], axis=1).astype(BF16)


def _prep_w_uq(w_uq):
    def per_head(part):
        pad = LANES - part.shape[-1]
        return jnp.pad(part, ((0, 0), (0, 0), (0, pad))).reshape(Q_LORA, N_HEADS * LANES).astype(BF16)

    rope = w_uq[..., QK_NOPE:]
    return per_head(w_uq[..., :QK_NOPE]), per_head(rope), per_head(_swap_halves(rope))


def _prep_w_uk(w_uk):
    wt = jnp.transpose(w_uk, (1, 2, 0))
    return jnp.pad(wt, ((0, 0), (0, LANES - QK_NOPE), (0, 0))).astype(BF16)


def _prep_w_uv(w_uv):
    eye = jnp.eye(N_HEADS, dtype=w_uv.dtype)
    wbd = jnp.einsum('rhd,hg->hrgd', w_uv, eye)
    return wbd.reshape(LAT_COLS, ATTN_WIDTH).astype(BF16)


def _prep_w_pool(w_pool):
    n = len(POOL_WINDOWS)
    eye = jnp.eye(n, dtype=w_pool.dtype)
    return jnp.einsum('gcd,gk->gckd', w_pool, eye).reshape(POOL_WIDTH, POOL_WIDTH).astype(BF16)


def _rope_tables():
    inv = ROPE_THETA ** (-jnp.arange(0, QK_ROPE, 2, dtype=F32) / QK_ROPE)
    pos = jnp.concatenate([jnp.tile(jnp.arange(SEQ), BATCH),
                           jnp.tile(PAST_LEN + jnp.arange(DEC_SEQ), DEC_BATCH)])
    ang = pos.astype(F32)[:, None] * inv[None, :]
    cos, sin = jnp.cos(ang), jnp.sin(ang)
    cos_t = _pad_cols(jnp.concatenate([cos, cos], axis=1), LANES)
    sin_t = _pad_cols(jnp.concatenate([-sin, sin], axis=1), LANES)
    return cos_t, sin_t


def _split_hi_lo(w):
    hi = w.astype(BF16)
    lo = (w - hi.astype(F32)).astype(BF16)
    return hi, lo


def kernel(x_prompt, x_sample, cache_ckv, cache_krope, page_table, state_pool, state_conv, g_mix_norm, w_in, b_gate, w_pool, s_pool, g_q_lat, w_uq, g_kv_lat, w_uk, w_uv, w_dw, b_dw, g_conv_ln, b_conv_ln, w_br, w_out, g_ffn_norm, w_d_gate, w_d_up, w_d_down, w_router, w_e_gate, w_e_up, w_e_down, g_final):
    x = jnp.concatenate([x_prompt.reshape(N_PROMPT, D_MODEL),
                         x_sample.reshape(N_SAMPLE, D_MODEL)], axis=0)
    cos_t, sin_t = _rope_tables()
    pt_flat = page_table.reshape(-1)
    cache_krope_t = jnp.swapaxes(cache_krope, 2, 3)
    expert_w = (w_e_gate.astype(BF16), w_e_up.astype(BF16), w_e_down.astype(BF16))

    ckv_p, kr_p, pool_p, conv_p = [], [], [], []
    ckv_s, kr_s, pool_s, conv_s = [], [], [], []
    for l in range(DEPTH):
        u_pool, c_q, kvb, c_kv, k_r, z, gates = _inproj(
            x, g_mix_norm[l][None], _prep_w_in(w_in[l]), b_gate[l][None],
            g_q_lat[l][None], g_kv_lat[l][None], cos_t, sin_t)

        q_args = (c_q, *_prep_w_uq(w_uq[l]), _prep_w_uk(w_uk[l]), cos_t, sin_t)
        qt_p = _qproj(True, *q_args)
        q_s = _qproj(False, *q_args)

        wbd = _prep_w_pool(w_pool[l])
        sp = s_pool[l][None]
        u_p = u_pool[:N_PROMPT].reshape(BATCH, SEQ, POOL_WIDTH)
        u_s = u_pool[N_PROMPT:].reshape(DEC_BATCH, DEC_SEQ, POOL_WIDTH)
        pool_ext = jnp.concatenate(
            [jnp.zeros((DEC_BATCH, POOL_HALO - POOL_BUF, POOL_WIDTH), F32), state_pool[l], u_s], axis=1)
        o_pool = (_pool_prompt(u_p, wbd, sp).reshape(N_PROMPT, POOL_WIDTH),
                  _pool_sample(pool_ext, wbd, sp).reshape(N_SAMPLE, POOL_WIDTH))

        z_p = z[:N_PROMPT].reshape(BATCH, SEQ, CONV_WIDTH)
        z_s = z[N_PROMPT:].reshape(DEC_BATCH, DEC_SEQ, CONV_WIDTH)
        conv_ext = jnp.concatenate(
            [jnp.zeros((DEC_BATCH, CONV_HALO - CONV_BUF, CONV_WIDTH), F32), state_conv[l], z_s], axis=1)
        conv_args = (w_dw[l], b_dw[l][None], g_conv_ln[l][None], b_conv_ln[l][None])
        o_conv = (_conv_prompt(z_p, *conv_args).reshape(N_PROMPT, CONV_WIDTH),
                  _conv_sample(conv_ext, *conv_args).reshape(N_SAMPLE, CONV_WIDTH))

        nblk = SEQ // TQ
        kv_p = kvb[:N_PROMPT].reshape(BATCH, nblk, TQ, Q_HEAD)
        vt_p = jnp.concatenate([jnp.swapaxes(kv_p[..., :KV_LORA], 2, 3),
                                jnp.ones((BATCH, nblk, Q_HEAD - KV_LORA, TQ), BF16)], axis=2)
        lat_p = _attn_prompt(qt_p, kv_p, vt_p)
        q_s = q_s.reshape(DEC_BATCH, DEC_SEQ, N_HEADS, Q_HEAD)
        q_s = jnp.swapaxes(q_s, 1, 2).reshape(DEC_BATCH, N_HEADS * DEC_SEQ, Q_HEAD)
        kv_s = kvb[N_PROMPT:].reshape(DEC_BATCH, DEC_SEQ, Q_HEAD)
        kc_new = jnp.pad(kv_s[..., :KV_LORA], ((0, 0), (0, LANES - DEC_SEQ), (0, 0)))
        krt_new = jnp.pad(jnp.swapaxes(kv_s[..., KV_LORA:KV_LORA + QK_ROPE], 1, 2),
                          ((0, 0), (0, 0), (0, LANES - DEC_SEQ)))
        lat_s = _attn_sample(l, pt_flat, q_s, kc_new, krt_new, cache_ckv, cache_krope_t)
        lat_s = jnp.swapaxes(lat_s.reshape(DEC_BATCH, N_HEADS, DEC_SEQ, KV_LORA), 1, 2)
        lat = (lat_p.reshape(N_PROMPT, LAT_COLS), lat_s.reshape(N_SAMPLE, LAT_COLS))

        is_moe = l % 2 == 1
        wr = _pad_cols(w_router[l // 2], LANES) if is_moe else jnp.zeros((D_MODEL, LANES), F32)
        wrh, wrl = _split_hi_lo(wr)
        x, h, route = _merge(is_moe, o_pool, lat, o_conv, gates, x, _prep_w_uv(w_uv[l]),
                             w_br[l].astype(BF16), w_out[l].astype(BF16), g_ffn_norm[l][None],
                             wrh, wrl)
        if is_moe:
            pos, tile_expert, n_active = _route(route)
            xs = _dispatch(pos, h, jnp.zeros((M_SORTED,) + ROW_TILE, F32))
            ys = _expert_ffn(l // 2, tile_expert, n_active, xs, *expert_w)
            x = _combine(pos, ys, x, route)
        else:
            x = _ffn(h, w_d_gate[l // 2].astype(BF16), w_d_up[l // 2].astype(BF16),
                     w_d_down[l // 2].astype(BF16), x)

        ckv_p.append(c_kv[:N_PROMPT].reshape(BATCH, SEQ, KV_LORA))
        kr_p.append(k_r[:N_PROMPT].reshape(BATCH, SEQ, QK_ROPE))
        pool_p.append(u_p[:, -POOL_BUF:])
        conv_p.append(z_p[:, -CONV_BUF:])
        ckv_s.append(c_kv[N_PROMPT:].reshape(DEC_BATCH, DEC_SEQ, KV_LORA))
        kr_s.append(k_r[N_PROMPT:].reshape(DEC_BATCH, DEC_SEQ, QK_ROPE))
        pool_s.append(jnp.concatenate([state_pool[l], u_s], axis=1)[:, -POOL_BUF:])
        conv_s.append(jnp.concatenate([state_conv[l], z_s], axis=1)[:, -CONV_BUF:])

    y = _final_norm(x, g_final[None])
    return (y[:N_PROMPT].reshape(BATCH, SEQ, D_MODEL),
            y[N_PROMPT:].reshape(DEC_BATCH, DEC_SEQ, D_MODEL),
            jnp.stack(ckv_p), jnp.stack(kr_p), jnp.stack(pool_p), jnp.stack(conv_p),
            jnp.stack(ckv_s), jnp.stack(kr_s), jnp.stack(pool_s), jnp.stack(conv_s))
```

```python
import functools

import jax
import jax.numpy as jnp
from jax import lax
from jax.experimental import pallas as pl
from jax.experimental.pallas import tpu as pltpu

F32 = jnp.float32
BF16 = jnp.bfloat16

D_MODEL = 1024
BATCH = 8
SEQ = 2048
DEPTH = 4
DEC_BATCH = 128
DEC_SEQ = 8
PAST_LEN = 8192
PAGE_SIZE = 128
N_PAGES = PAST_LEN // PAGE_SIZE

POOL_WINDOWS = (2, 4, 8, 16)
POOL_GROUP = 64
POOL_WIDTH = 256
POOL_BUF = 15

N_HEADS = 8
QK_NOPE = 64
QK_ROPE = 32
V_DIM = 64
Q_LORA = 256
KV_LORA = 128
ROPE_THETA = 10000.0
ATTN_WIDTH = N_HEADS * V_DIM
ATTN_SCALE = (QK_NOPE + QK_ROPE) ** -0.5

CONV_WIDTH = 256
CONV_K = 31
CONV_BUF = CONV_K - 1

N_BRANCH = 3
OFF_POOL = 0
OFF_Q = OFF_POOL + POOL_WIDTH
OFF_KV = OFF_Q + Q_LORA
OFF_KR = OFF_KV + KV_LORA
OFF_GLU = OFF_KR + QK_ROPE
OFF_GATE = OFF_GLU + 2 * CONV_WIDTH

D_FF = 2816
N_EXPERTS = 8
D_FF_EXPERT = 3584

EPS = 1e-6
NEG_INF = -1e30

N_PROMPT = BATCH * SEQ
N_SAMPLE = DEC_BATCH * DEC_SEQ
N_TOK = N_PROMPT + N_SAMPLE

LANES = 128
VMEM_BYTES_V7X = 64 * 1024 * 1024

C_POOL = 0
C_Q = C_POOL + POOL_WIDTH
C_KV = C_Q + Q_LORA
C_KRA = C_KV + KV_LORA
C_KRB = C_KRA + LANES
C_GLU = C_KRB + LANES
C_GATE = C_GLU + 2 * CONV_WIDTH
C_END = C_GATE + N_BRANCH * D_MODEL

Q_HEAD = 2 * LANES
Q_COLS = N_HEADS * Q_HEAD
LAT_COLS = N_HEADS * KV_LORA

TM_TOKEN = 512
TM_DENSE = 512
TF_DENSE = 1408
TM_EXPERT = 512
TF_EXPERT = 1792
TOP_K = 2
N_PAIRS = TOP_K * N_TOK
EXPERT_TILES = N_PAIRS // TM_EXPERT + N_EXPERTS
M_SORTED = EXPERT_TILES * TM_EXPERT
ROW_TILE = (8, LANES)
DMA_CHUNK = 512
PAGE_BUFFERS = 3
TM_NORM = 1024
TQ = 256
ATTN_ROWS = N_HEADS * TQ
LOG2_E = 1.4426950408889634
Q_SCALE = ATTN_SCALE * LOG2_E
POOL_HALO = 16
CONV_HALO = 32
SEQ_CHUNK = 256
CONV_ROWS = 64
SAMPLE_BB = 16


def _cparams(semantics, vmem_mib):
    assert vmem_mib * 1024 * 1024 < VMEM_BYTES_V7X
    return pltpu.CompilerParams(dimension_semantics=semantics,
                                vmem_limit_bytes=vmem_mib * 1024 * 1024)


def _rms(x, g):
    return x * lax.rsqrt(jnp.mean(x * x, axis=-1, keepdims=True) + EPS) * g


def _dot(a, b):
    return jnp.dot(a, b, preferred_element_type=F32)


def _dot_nt(a, b):
    return lax.dot_general(a, b, (((1,), (1,)), ((), ())), preferred_element_type=F32)


def _inproj_kernel(x_ref, g_ref, w_ref, bg_ref, gq_ref, gkv_ref, cos_ref, sin_ref,
                   up_ref, cq_ref, kvb_ref, ckv_ref, kr_ref, z_ref, gate_ref):
    h = _rms(x_ref[...], g_ref[...]).astype(BF16)

    def proj(lo, hi):
        return _dot(h, w_ref[:, lo:hi])

    up_ref[...] = proj(C_POOL, C_Q)
    cq_ref[...] = _rms(proj(C_Q, C_KV), gq_ref[...]).astype(BF16)
    ckv = _rms(proj(C_KV, C_KRA), gkv_ref[...])
    kr = proj(C_KRA, C_KRB) * cos_ref[...] + proj(C_KRB, C_GLU) * sin_ref[...]
    ckv_ref[...] = ckv
    kr_ref[...] = kr[:, :QK_ROPE]
    kvb_ref[:, :KV_LORA] = ckv.astype(BF16)
    kvb_ref[:, KV_LORA:] = kr.astype(BF16)
    glu = proj(C_GLU, C_GATE)
    z_ref[...] = glu[:, :CONV_WIDTH] * jax.nn.sigmoid(glu[:, CONV_WIDTH:])
    for c in range(N_BRANCH):
        lo = c * D_MODEL
        g = proj(C_GATE + lo, C_GATE + lo + D_MODEL) + bg_ref[:, lo:lo + D_MODEL]
        gate_ref[:, lo:lo + D_MODEL] = jax.nn.sigmoid(g).astype(BF16)


def _inproj(x, g, w, bg, gq, gkv, cos, sin):
    tm = TM_TOKEN
    row = lambda i: (i, 0)
    fix = lambda i: (0, 0)
    return pl.pallas_call(
        _inproj_kernel,
        grid=(N_TOK // tm,),
        in_specs=[
            pl.BlockSpec((tm, D_MODEL), row),
            pl.BlockSpec((1, D_MODEL), fix),
            pl.BlockSpec((D_MODEL, C_END), fix),
            pl.BlockSpec((1, N_BRANCH * D_MODEL), fix),
            pl.BlockSpec((1, Q_LORA), fix),
            pl.BlockSpec((1, KV_LORA), fix),
            pl.BlockSpec((tm, LANES), row),
            pl.BlockSpec((tm, LANES), row),
        ],
        out_specs=[
            pl.BlockSpec((tm, POOL_WIDTH), row),
            pl.BlockSpec((tm, Q_LORA), row),
            pl.BlockSpec((tm, Q_HEAD), row),
            pl.BlockSpec((tm, KV_LORA), row),
            pl.BlockSpec((tm, QK_ROPE), row),
            pl.BlockSpec((tm, CONV_WIDTH), row),
            pl.BlockSpec((tm, N_BRANCH * D_MODEL), row),
        ],
        out_shape=[
            jax.ShapeDtypeStruct((N_TOK, POOL_WIDTH), F32),
            jax.ShapeDtypeStruct((N_TOK, Q_LORA), BF16),
            jax.ShapeDtypeStruct((N_TOK, Q_HEAD), BF16),
            jax.ShapeDtypeStruct((N_TOK, KV_LORA), F32),
            jax.ShapeDtypeStruct((N_TOK, QK_ROPE), F32),
            jax.ShapeDtypeStruct((N_TOK, CONV_WIDTH), F32),
            jax.ShapeDtypeStruct((N_TOK, N_BRANCH * D_MODEL), BF16),
        ],
        compiler_params=_cparams(("parallel",), 56),
        name="inproj",
    )(x, g, w, bg, gq, gkv, cos, sin)


def _q_heads(cq_ref, wn_ref, wa_ref, wb_ref, wuk_ref, cos_ref, sin_ref):
    cq = cq_ref[...]
    qn = _dot(cq, wn_ref[...]).astype(BF16)
    ra = _dot(cq, wa_ref[...])
    rb = _dot(cq, wb_ref[...])
    cos = cos_ref[...]
    sin = sin_ref[...]
    for h in range(N_HEADS):
        sl = slice(h * LANES, (h + 1) * LANES)
        yield (h, _dot(qn[:, sl], wuk_ref[h]) * Q_SCALE,
               (ra[:, sl] * cos + rb[:, sl] * sin) * Q_SCALE)


def _qproj_sample_kernel(*refs):
    q_ref = refs[-1]
    for h, q_abs, q_rope in _q_heads(*refs[:-1]):
        q_ref[:, h * Q_HEAD:h * Q_HEAD + LANES] = q_abs.astype(BF16)
        q_ref[:, h * Q_HEAD + LANES:(h + 1) * Q_HEAD] = q_rope.astype(BF16)


def _qproj_prompt_kernel(*refs):
    qt_ref = refs[-1]
    for h, q_abs, q_rope in _q_heads(*refs[:-1]):
        for c in range(TM_TOKEN // TQ):
            rows = slice(c * TQ, (c + 1) * TQ)
            cols = slice(h * TQ, (h + 1) * TQ)
            qt_ref[0, c, :LANES, cols] = q_abs[rows].T.astype(BF16)
            qt_ref[0, c, LANES:, cols] = q_rope[rows].T.astype(BF16)


def _qproj(for_prompt, cq, wn, wa, wb, wuk, cos, sin):
    tm = TM_TOKEN
    fix = lambda i: (0, 0)
    if for_prompt:
        n_steps, first = N_PROMPT // tm, 0
        tiles = tm // TQ
        per_seq = SEQ // tm
        out_spec = pl.BlockSpec((1, tiles, Q_HEAD, ATTN_ROWS),
                                lambda i: (i // per_seq, i % per_seq, 0, 0))
        out_shape = jax.ShapeDtypeStruct((BATCH, SEQ // TQ, Q_HEAD, ATTN_ROWS), BF16)
    else:
        n_steps, first = N_SAMPLE // tm, N_PROMPT // tm
        out_spec = pl.BlockSpec((tm, Q_COLS), lambda i: (i, 0))
        out_shape = jax.ShapeDtypeStruct((N_SAMPLE, Q_COLS), BF16)
    row = lambda i: (first + i, 0)
    return pl.pallas_call(
        _qproj_prompt_kernel if for_prompt else _qproj_sample_kernel,
        grid=(n_steps,),
        in_specs=[
            pl.BlockSpec((tm, Q_LORA), row),
            pl.BlockSpec((Q_LORA, N_HEADS * LANES), fix),
            pl.BlockSpec((Q_LORA, N_HEADS * LANES), fix),
            pl.BlockSpec((Q_LORA, N_HEADS * LANES), fix),
            pl.BlockSpec((N_HEADS, LANES, KV_LORA), lambda i: (0, 0, 0)),
            pl.BlockSpec((tm, LANES), row),
            pl.BlockSpec((tm, LANES), row),
        ],
        out_specs=out_spec,
        out_shape=out_shape,
        compiler_params=_cparams(("parallel",), 40),
        name="qproj_prompt" if for_prompt else "qproj_sample",
    )(cq, wn, wa, wb, wuk, cos, sin)


def _pool_compute(ext_ref, n_rows, cnt_of_window, wbd_ref, sp_ref):
    def ld(j):
        return ext_ref[:, pl.ds(POOL_HALO - j, n_rows), :]

    tok = ld(0)
    run = tok
    sums = {}
    for j in range(1, max(POOL_WINDOWS)):
        run = run + ld(j)
        if j + 1 in POOL_WINDOWS:
            sums[j + 1] = run
    lane = lax.broadcasted_iota(jnp.int32, tok.shape, 2)
    pooled = sums[POOL_WINDOWS[-1]] / cnt_of_window(POOL_WINDOWS[-1])
    for g in range(len(POOL_WINDOWS) - 2, -1, -1):
        w = POOL_WINDOWS[g]
        pooled = jnp.where(lane < (g + 1) * POOL_GROUP, sums[w] / cnt_of_window(w), pooled)
    pooled = (pooled - tok).reshape(-1, POOL_WIDTH).astype(BF16)
    return _dot(pooled, wbd_ref[...]) * sp_ref[...]


def _pool_prompt_kernel(halo_ref, u_ref, wbd_ref, sp_ref, o_ref, ext_ref):
    i = pl.program_id(1)
    ext_ref[:, :POOL_HALO, :] = jnp.where(i > 0, halo_ref[...], 0.0)
    ext_ref[:, POOL_HALO:, :] = u_ref[...]
    pos = lax.broadcasted_iota(jnp.int32, (1, SEQ_CHUNK, 1), 1) + i * SEQ_CHUNK

    def cnt(w):
        return jnp.minimum(pos + 1, w).astype(F32)

    o_ref[0] = _pool_compute(ext_ref, SEQ_CHUNK, cnt, wbd_ref, sp_ref).astype(BF16)


def _pool_prompt(u, wbd, sp):
    hb = SEQ_CHUNK // POOL_HALO
    return pl.pallas_call(
        _pool_prompt_kernel,
        grid=(BATCH, SEQ // SEQ_CHUNK),
        in_specs=[
            pl.BlockSpec((1, POOL_HALO, POOL_WIDTH),
                         lambda b, i: (b, jnp.maximum(i * hb - 1, 0), 0)),
            pl.BlockSpec((1, SEQ_CHUNK, POOL_WIDTH), lambda b, i: (b, i, 0)),
            pl.BlockSpec((POOL_WIDTH, POOL_WIDTH), lambda b, i: (0, 0)),
            pl.BlockSpec((1, POOL_WIDTH), lambda b, i: (0, 0)),
        ],
        out_specs=pl.BlockSpec((1, SEQ_CHUNK, POOL_WIDTH), lambda b, i: (b, i, 0)),
        out_shape=jax.ShapeDtypeStruct((BATCH, SEQ, POOL_WIDTH), BF16),
        scratch_shapes=[pltpu.VMEM((1, POOL_HALO + SEQ_CHUNK, POOL_WIDTH), F32)],
        compiler_params=_cparams(("parallel", "parallel"), 32),
        name="pool_prompt",
    )(u, u, wbd, sp)


def _pool_sample_kernel(ext_ref, wbd_ref, sp_ref, o_ref):
    out = _pool_compute(ext_ref, DEC_SEQ, lambda w: float(w), wbd_ref, sp_ref)
    o_ref[...] = out.reshape(SAMPLE_BB, DEC_SEQ, POOL_WIDTH).astype(BF16)


def _pool_sample(ext, wbd, sp):
    return pl.pallas_call(
        _pool_sample_kernel,
        grid=(DEC_BATCH // SAMPLE_BB,),
        in_specs=[
            pl.BlockSpec((SAMPLE_BB, POOL_HALO + DEC_SEQ, POOL_WIDTH), lambda i: (i, 0, 0)),
            pl.BlockSpec((POOL_WIDTH, POOL_WIDTH), lambda i: (0, 0)),
            pl.BlockSpec((1, POOL_WIDTH), lambda i: (0, 0)),
        ],
        out_specs=pl.BlockSpec((SAMPLE_BB, DEC_SEQ, POOL_WIDTH), lambda i: (i, 0, 0)),
        out_shape=jax.ShapeDtypeStruct((DEC_BATCH, DEC_SEQ, POOL_WIDTH), BF16),
        compiler_params=_cparams(("parallel",), 32),
        name="pool_sample",
    )(ext, wbd, sp)


def _conv_compute(ext_ref, row0, n_rows, wdw_ref, bdw_ref, g_ref, b_ref):
    lead = CONV_HALO - CONV_BUF
    acc = None
    for k in range(CONV_K):
        term = ext_ref[:, pl.ds(row0 + lead + k, n_rows), :] * wdw_ref[k:k + 1, :]
        acc = term if acc is None else acc + term
    zc = acc + bdw_ref[...]
    mu = jnp.mean(zc, axis=-1, keepdims=True)
    xc = zc - mu
    y = xc * lax.rsqrt(jnp.mean(xc * xc, axis=-1, keepdims=True) + EPS)
    y = y * g_ref[...] + b_ref[...]
    return y * jax.nn.sigmoid(y)


def _conv_prompt_kernel(halo_ref, z_ref, wdw_ref, bdw_ref, g_ref, b_ref, o_ref, ext_ref):
    i = pl.program_id(1)
    ext_ref[:, :CONV_HALO, :] = jnp.where(i > 0, halo_ref[...], 0.0)
    ext_ref[:, CONV_HALO:, :] = z_ref[...]
    for r0 in range(0, SEQ_CHUNK, CONV_ROWS):
        o_ref[:, r0:r0 + CONV_ROWS, :] = _conv_compute(
            ext_ref, r0, CONV_ROWS, wdw_ref, bdw_ref, g_ref, b_ref).astype(BF16)


def _conv_prompt(z, wdw, bdw, g, b):
    hb = SEQ_CHUNK // CONV_HALO
    vec = pl.BlockSpec((1, CONV_WIDTH), lambda bb, i: (0, 0))
    return pl.pallas_call(
        _conv_prompt_kernel,
        grid=(BATCH, SEQ // SEQ_CHUNK),
        in_specs=[
            pl.BlockSpec((1, CONV_HALO, CONV_WIDTH),
                         lambda bb, i: (bb, jnp.maximum(i * hb - 1, 0), 0)),
            pl.BlockSpec((1, SEQ_CHUNK, CONV_WIDTH), lambda bb, i: (bb, i, 0)),
            pl.BlockSpec((CONV_K, CONV_WIDTH), lambda bb, i: (0, 0)),
            vec, vec, vec,
        ],
        out_specs=pl.BlockSpec((1, SEQ_CHUNK, CONV_WIDTH), lambda bb, i: (bb, i, 0)),
        out_shape=jax.ShapeDtypeStruct((BATCH, SEQ, CONV_WIDTH), BF16),
        scratch_shapes=[pltpu.VMEM((1, CONV_HALO + SEQ_CHUNK, CONV_WIDTH), F32)],
        compiler_params=_cparams(("parallel", "parallel"), 32),
        name="conv_prompt",
    )(z, z, wdw, bdw, g, b)


def _conv_sample_kernel(ext_ref, wdw_ref, bdw_ref, g_ref, b_ref, o_ref):
    o_ref[...] = _conv_compute(ext_ref, 0, DEC_SEQ, wdw_ref, bdw_ref, g_ref, b_ref).astype(BF16)


def _conv_sample(ext, wdw, bdw, g, b):
    vec = pl.BlockSpec((1, CONV_WIDTH), lambda i: (0, 0))
    return pl.pallas_call(
        _conv_sample_kernel,
        grid=(DEC_BATCH // SAMPLE_BB,),
        in_specs=[
            pl.BlockSpec((SAMPLE_BB, CONV_HALO + DEC_SEQ, CONV_WIDTH), lambda i: (i, 0, 0)),
            pl.BlockSpec((CONV_K, CONV_WIDTH), lambda i: (0, 0)),
            vec, vec, vec,
        ],
        out_specs=pl.BlockSpec((SAMPLE_BB, DEC_SEQ, CONV_WIDTH), lambda i: (i, 0, 0)),
        out_shape=jax.ShapeDtypeStruct((DEC_BATCH, DEC_SEQ, CONV_WIDTH), BF16),
        compiler_params=_cparams(("parallel",), 32),
        name="conv_sample",
    )(ext, wdw, bdw, g, b)


def _attn_prompt_kernel(qt_ref, k_ref, vt_ref, o_ref, m_ref, acc_ref):
    qi = pl.program_id(1)
    qt = qt_ref[0, 0]
    m_ref[...] = jnp.full(m_ref.shape, NEG_INF, F32)
    acc_ref[...] = jnp.zeros(acc_ref.shape, F32)

    def block(j, on_diagonal):
        s = _dot(k_ref[0, j], qt)
        if on_diagonal:
            tok = lax.broadcasted_iota(jnp.int32, s.shape, 1) & (TQ - 1)
            key = lax.broadcasted_iota(jnp.int32, s.shape, 0)
            s = jnp.where(key <= tok, s, NEG_INF)
        m_old = m_ref[...]
        m_new = jnp.maximum(m_old, jnp.max(s, axis=0, keepdims=True))
        p = jnp.exp2(s - m_new).astype(BF16)
        acc_ref[...] = jnp.exp2(m_old - m_new) * acc_ref[...] + _dot(vt_ref[0, j], p)
        m_ref[...] = m_new

    def below_diagonal(j, carry):
        block(j, False)
        return carry

    lax.fori_loop(0, qi, below_diagonal, 0)
    block(qi, True)
    acc = acc_ref[...]
    out = (acc[:KV_LORA] / acc[KV_LORA:]).T.astype(BF16)
    for h in range(N_HEADS):
        o_ref[0, :, h * KV_LORA:(h + 1) * KV_LORA] = out[h * TQ:(h + 1) * TQ]


def _attn_prompt(qt, k, vt):
    nblk = SEQ // TQ
    return pl.pallas_call(
        _attn_prompt_kernel,
        grid=(BATCH, nblk),
        in_specs=[
            pl.BlockSpec((1, 1, Q_HEAD, ATTN_ROWS), lambda b, i: (b, i, 0, 0)),
            pl.BlockSpec((1, nblk, TQ, Q_HEAD), lambda b, i: (b, 0, 0, 0)),
            pl.BlockSpec((1, nblk, Q_HEAD, TQ), lambda b, i: (b, 0, 0, 0)),
        ],
        out_specs=pl.BlockSpec((1, TQ, LAT_COLS), lambda b, i: (b, i, 0)),
        out_shape=jax.ShapeDtypeStruct((BATCH, SEQ, LAT_COLS), BF16),
        scratch_shapes=[pltpu.VMEM((1, ATTN_ROWS), F32), pltpu.VMEM((Q_HEAD, ATTN_ROWS), F32)],
        compiler_params=_cparams(("parallel", "arbitrary"), 40),
        name="attn_prompt",
    )(qt, k, vt)


def _attn_sample_kernel(layer, pt_ref, q_ref, kcn_ref, krn_ref, ckv_hbm, krt_hbm, o_ref, *scratch):
    b = pl.program_id(0)
    n = pl.num_programs(0)
    ckbufs = scratch[:PAGE_BUFFERS]
    krbufs = scratch[PAGE_BUFFERS:2 * PAGE_BUFFERS]
    semc, semr = scratch[2 * PAGE_BUFFERS:]
    ahead = PAGE_BUFFERS - 1

    def issue(seq, slot):
        for p in range(N_PAGES):
            pid = pt_ref[seq * N_PAGES + p]
            pltpu.make_async_copy(ckv_hbm.at[layer, pid], ckbufs[slot].at[p],
                                  semc.at[slot]).start(priority=p % 2)
            pltpu.make_async_copy(krt_hbm.at[layer, pid], krbufs[slot].at[p],
                                  semr.at[slot]).start(priority=p % 2)

    def drain(slot):
        pltpu.make_async_copy(ckv_hbm.at[layer, pl.ds(0, N_PAGES)], ckbufs[slot], semc.at[slot]).wait()
        pltpu.make_async_copy(krt_hbm.at[layer, pl.ds(0, N_PAGES)], krbufs[slot], semr.at[slot]).wait()

    @pl.when(b == 0)
    def _():
        for first in range(ahead):
            issue(first, first)

    def update(state, s, kc):
        m, l, acc = state
        m_new = jnp.maximum(m, jnp.max(s, axis=-1, keepdims=True))
        alpha = jnp.exp2(m - m_new)
        p = jnp.exp2(s - m_new)
        return (m_new, alpha * l + jnp.sum(p, axis=-1, keepdims=True),
                alpha * acc + _dot(p.astype(BF16), kc))

    def step(slot):
        drain(slot)
        issue(jnp.minimum(b + ahead, n - 1), (slot + ahead) % PAGE_BUFFERS)
        q = q_ref[0]
        qa = q[:, :KV_LORA]
        qr = q[:, KV_LORA:KV_LORA + QK_ROPE]
        rows = N_HEADS * DEC_SEQ
        state = (jnp.full((rows, 1), NEG_INF, F32), jnp.zeros((rows, 1), F32),
                 jnp.zeros((rows, KV_LORA), F32))
        kc = ckbufs[slot][...].reshape(PAST_LEN, KV_LORA).astype(BF16)
        kr = jnp.concatenate([krbufs[slot][p] for p in range(N_PAGES)], axis=-1).astype(BF16)
        state = update(state, _dot_nt(qa, kc) + _dot(qr, kr), kc)
        kcn = kcn_ref[0]
        s = _dot_nt(qa, kcn) + _dot(qr, krn_ref[0])
        t = lax.broadcasted_iota(jnp.int32, s.shape, 0) & (DEC_SEQ - 1)
        key = lax.broadcasted_iota(jnp.int32, s.shape, 1)
        _, l, acc = update(state, jnp.where(key <= t, s, NEG_INF), kcn)
        o_ref[0] = (acc / l).astype(BF16)

        @pl.when(b == n - 1)
        def _():
            for later in range(1, PAGE_BUFFERS):
                drain((slot + later) % PAGE_BUFFERS)

    for slot in range(PAGE_BUFFERS):
        pl.when(b % PAGE_BUFFERS == slot)(functools.partial(step, slot))


def _attn_sample(layer, page_table, q, kc_new, krt_new, cache_ckv, cache_krope_t):
    page_ck = pltpu.VMEM((N_PAGES, PAGE_SIZE, KV_LORA), F32)
    page_kr = pltpu.VMEM((N_PAGES, QK_ROPE, PAGE_SIZE), F32)
    grid_spec = pltpu.PrefetchScalarGridSpec(
        num_scalar_prefetch=1,
        grid=(DEC_BATCH,),
        in_specs=[pl.BlockSpec((1, N_HEADS * DEC_SEQ, Q_HEAD), lambda b, pt: (b, 0, 0)),
                  pl.BlockSpec((1, LANES, KV_LORA), lambda b, pt: (b, 0, 0)),
                  pl.BlockSpec((1, QK_ROPE, LANES), lambda b, pt: (b, 0, 0)),
                  pl.BlockSpec(memory_space=pl.ANY),
                  pl.BlockSpec(memory_space=pl.ANY)],
        out_specs=pl.BlockSpec((1, N_HEADS * DEC_SEQ, KV_LORA), lambda b, pt: (b, 0, 0)),
        scratch_shapes=([page_ck] * PAGE_BUFFERS + [page_kr] * PAGE_BUFFERS
                        + [pltpu.SemaphoreType.DMA((PAGE_BUFFERS,))] * 2),
    )
    return pl.pallas_call(
        functools.partial(_attn_sample_kernel, layer),
        grid_spec=grid_spec,
        out_shape=jax.ShapeDtypeStruct((DEC_BATCH, N_HEADS * DEC_SEQ, KV_LORA), BF16),
        compiler_params=_cparams(("arbitrary",), 48),
        name="attn_sample",
    )(page_table, q, kc_new, krt_new, cache_ckv, cache_krope_t)


def _merge_kernel(with_router, op_p, op_s, lat_p, lat_s, oc_p, oc_s, gate_ref, x_ref, wuv_ref,
                  wbr_ref, wout_ref, gf_ref, wrh_ref, wrl_ref, xo_ref, h_ref, route_ref):
    in_prompt = pl.program_id(0) < N_PROMPT // TM_TOKEN

    def pick(prompt_ref, sample_ref):
        return jnp.where(in_prompt, prompt_ref[...], sample_ref[...])

    o_attn = _dot(pick(lat_p, lat_s), wuv_ref[...]).astype(BF16)
    br_a = _dot(pick(op_p, op_s), wbr_ref[:POOL_WIDTH, :])
    br_b = _dot(o_attn, wbr_ref[POOL_WIDTH:POOL_WIDTH + ATTN_WIDTH, :])
    br_c = _dot(pick(oc_p, oc_s), wbr_ref[POOL_WIDTH + ATTN_WIDTH:, :])
    merged = (gate_ref[:, :D_MODEL].astype(F32) * br_a
              + gate_ref[:, D_MODEL:2 * D_MODEL].astype(F32) * br_b
              + gate_ref[:, 2 * D_MODEL:].astype(F32) * br_c)
    xn = x_ref[...] + _dot(merged.astype(BF16), wout_ref[...])
    xo_ref[...] = xn
    hn = _rms(xn, gf_ref[...])
    if not with_router:
        h_ref[...] = hn.astype(BF16)
        route_ref[...] = jnp.zeros(route_ref.shape, F32)
        return
    for c in range(ROW_TILE[0]):
        h_ref[:, c, :] = hn[:, c * LANES:(c + 1) * LANES]
    hh = hn.astype(BF16)
    hl = (hn - hh.astype(F32)).astype(BF16)
    lg = _dot(hh, wrh_ref[...]) + _dot(hl, wrh_ref[...]) + _dot(hh, wrl_ref[...])
    lane = lax.broadcasted_iota(jnp.int32, lg.shape, 1).astype(F32)
    lg = jnp.where(lane < N_EXPERTS, lg, NEG_INF)
    m1 = jnp.max(lg, axis=-1, keepdims=True)
    i1 = jnp.min(jnp.where(lg == m1, lane, float(LANES)), axis=-1, keepdims=True)
    lg2 = jnp.where(lane == i1, NEG_INF, lg)
    m2 = jnp.max(lg2, axis=-1, keepdims=True)
    i2 = jnp.min(jnp.where(lg2 == m2, lane, float(LANES)), axis=-1, keepdims=True)
    e = jnp.exp(m2 - m1)
    w1 = 1.0 / (1.0 + e)
    w2 = e / (1.0 + e)
    route_ref[...] = jnp.where(lane == 0, i1, jnp.where(lane == 1, i2, jnp.where(
        lane == 2, w1, jnp.where(lane == 3, w2, 0.0))))


def _merge(with_router, o_pool, lat, o_conv, gates, x, wuv, wbr, wout, gf, wrh, wrl):
    tm = TM_TOKEN
    row = lambda i: (i, 0)
    fix = lambda i: (0, 0)
    prompt_tiles = N_PROMPT // tm
    prompt_row = lambda i: (jnp.minimum(i, prompt_tiles - 1), 0)
    sample_row = lambda i: (jnp.maximum(i - prompt_tiles, 0), 0)

    def pair(width):
        return [pl.BlockSpec((tm, width), prompt_row), pl.BlockSpec((tm, width), sample_row)]
    if with_router:
        h_spec = pl.BlockSpec((tm,) + ROW_TILE, lambda i: (i, 0, 0))
        h_shape = jax.ShapeDtypeStruct((N_TOK,) + ROW_TILE, F32)
    else:
        h_spec = pl.BlockSpec((tm, D_MODEL), row)
        h_shape = jax.ShapeDtypeStruct((N_TOK, D_MODEL), BF16)
    return pl.pallas_call(
        functools.partial(_merge_kernel, with_router),
        grid=(N_TOK // tm,),
        in_specs=pair(POOL_WIDTH) + pair(LAT_COLS) + pair(CONV_WIDTH) + [
            pl.BlockSpec((tm, N_BRANCH * D_MODEL), row),
            pl.BlockSpec((tm, D_MODEL), row),
            pl.BlockSpec((LAT_COLS, ATTN_WIDTH), fix),
            pl.BlockSpec((D_MODEL, D_MODEL), fix),
            pl.BlockSpec((D_MODEL, D_MODEL), fix),
            pl.BlockSpec((1, D_MODEL), fix),
            pl.BlockSpec((D_MODEL, LANES), fix),
            pl.BlockSpec((D_MODEL, LANES), fix),
        ],
        out_specs=[
            pl.BlockSpec((tm, D_MODEL), row),
            h_spec,
            pl.BlockSpec((tm, LANES), row),
        ],
        out_shape=[
            jax.ShapeDtypeStruct((N_TOK, D_MODEL), F32),
            h_shape,
            jax.ShapeDtypeStruct((N_TOK, LANES), F32),
        ],
        compiler_params=_cparams(("parallel",), 48),
        name="merge_router" if with_router else "merge",
    )(*o_pool, *lat, *o_conv, gates, x, wuv, wbr, wout, gf, wrh, wrl)


def _row_chunk_copy(src_ref, dst_ref, sem):
    return pltpu.make_async_copy(src_ref.at[pl.ds(0, DMA_CHUNK)], dst_ref.at[pl.ds(0, DMA_CHUNK)], sem)


def _dispatch_kernel(pos_ref, h_ref, init_ref, xs_ref, sem):
    del init_ref
    pair0 = pl.program_id(0) * DMA_CHUNK

    def issue(r, carry):
        pltpu.make_async_copy(h_ref.at[r], xs_ref.at[pos_ref[pair0 + r]], sem).start()
        return carry

    lax.fori_loop(0, DMA_CHUNK, issue, 0, unroll=8)
    _row_chunk_copy(h_ref, xs_ref, sem).wait()


def _dispatch(pos, h_rows, init):
    token_chunks = N_TOK // DMA_CHUNK
    grid_spec = pltpu.PrefetchScalarGridSpec(
        num_scalar_prefetch=1,
        grid=(N_PAIRS // DMA_CHUNK,),
        in_specs=[pl.BlockSpec((DMA_CHUNK,) + ROW_TILE,
                               lambda i, pos: (i % token_chunks, 0, 0)),
                  pl.BlockSpec(memory_space=pl.ANY)],
        out_specs=pl.BlockSpec(memory_space=pl.ANY),
        scratch_shapes=[pltpu.SemaphoreType.DMA(())],
    )
    return pl.pallas_call(
        _dispatch_kernel,
        grid_spec=grid_spec,
        out_shape=jax.ShapeDtypeStruct((M_SORTED,) + ROW_TILE, F32),
        input_output_aliases={2: 0},
        compiler_params=_cparams(("arbitrary",), 16),
        name="dispatch",
    )(pos, h_rows, init)


def _expert_ffn_kernel(te_ref, na_ref, xs_ref, wg_ref, wu_ref, wd_ref, ys_ref, xb_ref, acc_ref):
    del te_ref
    i = pl.program_id(0)
    f = pl.program_id(1)
    active = i < na_ref[0]

    @pl.when(active & (f == 0))
    def _():
        xb_ref[...] = jnp.concatenate(
            [xs_ref[:, c, :] for c in range(ROW_TILE[0])], axis=-1).astype(BF16)

    @pl.when(active)
    def _():
        x = xb_ref[...]
        a = _dot(x, wg_ref[...])
        b = _dot(x, wu_ref[...])
        y = _dot((a * jax.nn.sigmoid(a) * b).astype(BF16), wd_ref[...])

        @pl.when(f == 0)
        def _():
            acc_ref[...] = y

        @pl.when(f > 0)
        def _():
            acc_ref[...] += y

    @pl.when(f == pl.num_programs(1) - 1)
    def _():
        y = jnp.where(active, acc_ref[...], 0.0)
        for c in range(ROW_TILE[0]):
            ys_ref[:, c, :] = y[:, c * LANES:(c + 1) * LANES]


def _expert_ffn(moe_layer, tile_expert, n_active, xs, wg, wu, wd):
    tm, tf = TM_EXPERT, TF_EXPERT
    nf = D_FF_EXPERT // tf

    def tile(i, na):
        return jnp.minimum(i, na[0] - 1)

    def hidden(i, f, na):
        return jnp.where(i < na[0], f, nf - 1)

    grid_spec = pltpu.PrefetchScalarGridSpec(
        num_scalar_prefetch=2,
        grid=(EXPERT_TILES, nf),
        in_specs=[
            pl.BlockSpec((tm,) + ROW_TILE, lambda i, f, te, na: (tile(i, na), 0, 0)),
            pl.BlockSpec((None, None, D_MODEL, tf),
                         lambda i, f, te, na: (moe_layer, te[tile(i, na)], 0, hidden(i, f, na))),
            pl.BlockSpec((None, None, D_MODEL, tf),
                         lambda i, f, te, na: (moe_layer, te[tile(i, na)], 0, hidden(i, f, na))),
            pl.BlockSpec((None, None, tf, D_MODEL),
                         lambda i, f, te, na: (moe_layer, te[tile(i, na)], hidden(i, f, na), 0)),
        ],
        out_specs=pl.BlockSpec((tm,) + ROW_TILE, lambda i, f, te, na: (i, 0, 0)),
        scratch_shapes=[pltpu.VMEM((tm, D_MODEL), BF16), pltpu.VMEM((tm, D_MODEL), F32)],
    )
    return pl.pallas_call(
        _expert_ffn_kernel,
        grid_spec=grid_spec,
        out_shape=jax.ShapeDtypeStruct((M_SORTED,) + ROW_TILE, F32),
        compiler_params=_cparams(("arbitrary", "arbitrary"), 56),
        name="expert_ffn",
    )(tile_expert, n_active, xs, wg, wu, wd)


def _combine_kernel(pos_ref, ys_ref, x_ref, route_ref, o_ref, buf_ref, sem):
    i = pl.program_id(0)
    n = pl.num_programs(0)
    tmc = DMA_CHUNK // TOP_K

    def issue(tile, slot):
        def body(r, carry):
            for k in range(TOP_K):
                pltpu.make_async_copy(ys_ref.at[pos_ref[k * N_TOK + tile * tmc + r]],
                                      buf_ref.at[slot, k * tmc + r], sem.at[slot]).start()
            return carry

        lax.fori_loop(0, tmc, body, 0, unroll=8)

    @pl.when(i == 0)
    def _():
        issue(0, 0)

    @pl.when(i + 1 < n)
    def _():
        issue(i + 1, (i + 1) % 2)

    slot = i % 2
    _row_chunk_copy(ys_ref, buf_ref.at[slot], sem.at[slot]).wait()
    w1 = route_ref[:, 2:3]
    w2 = route_ref[:, 3:4]
    for c in range(ROW_TILE[0]):
        cols = slice(c * LANES, (c + 1) * LANES)
        o_ref[:, cols] = x_ref[:, cols] + (w1 * buf_ref[slot, :tmc, c, :]
                                            + w2 * buf_ref[slot, tmc:, c, :])


def _combine(pos, ys, x, route):
    tmc = DMA_CHUNK // TOP_K
    grid_spec = pltpu.PrefetchScalarGridSpec(
        num_scalar_prefetch=1,
        grid=(N_TOK // tmc,),
        in_specs=[
            pl.BlockSpec(memory_space=pl.ANY),
            pl.BlockSpec((tmc, D_MODEL), lambda i, pos: (i, 0)),
            pl.BlockSpec((tmc, LANES), lambda i, pos: (i, 0)),
        ],
        out_specs=pl.BlockSpec((tmc, D_MODEL), lambda i, pos: (i, 0)),
        scratch_shapes=[pltpu.VMEM((2, DMA_CHUNK) + ROW_TILE, F32),
                        pltpu.SemaphoreType.DMA((2,))],
    )
    return pl.pallas_call(
        _combine_kernel,
        grid_spec=grid_spec,
        out_shape=jax.ShapeDtypeStruct((N_TOK, D_MODEL), F32),
        compiler_params=_cparams(("arbitrary",), 32),
        name="combine",
    )(pos, ys, x, route)


def _route(route):
    experts = jnp.concatenate([route[:, 0], route[:, 1]]).astype(jnp.int32)
    one_hot = (experts[:, None] == jnp.arange(N_EXPERTS)[None, :]).astype(jnp.int32)
    running = jnp.cumsum(one_hot, axis=0)
    rank = jnp.sum(one_hot * running, axis=1) - 1
    tiles = (running[-1] + TM_EXPERT - 1) // TM_EXPERT
    tile_end = jnp.cumsum(tiles)
    pos = jnp.sum(one_hot * ((tile_end - tiles) * TM_EXPERT)[None, :], axis=1) + rank
    tile_ids = jnp.arange(EXPERT_TILES, dtype=jnp.int32)
    tile_expert = jnp.minimum(jnp.sum(tile_ids[:, None] >= tile_end[None, :], axis=1),
                              N_EXPERTS - 1)
    return pos.astype(jnp.int32), tile_expert.astype(jnp.int32), tile_end[-1:].astype(jnp.int32)


def _ffn_kernel(h_ref, wg_ref, wu_ref, wd_ref, x_ref, o_ref, acc_ref):
    f = pl.program_id(1)
    h = h_ref[...]
    a = _dot(h, wg_ref[...])
    b = _dot(h, wu_ref[...])
    y = _dot((a * jax.nn.sigmoid(a) * b).astype(BF16), wd_ref[...])

    @pl.when(f == 0)
    def _():
        acc_ref[...] = y

    @pl.when(f > 0)
    def _():
        acc_ref[...] += y

    @pl.when(f == pl.num_programs(1) - 1)
    def _():
        o_ref[...] = x_ref[...] + acc_ref[...]


def _ffn(h, wg, wu, wd, x):
    tm, tf = TM_DENSE, TF_DENSE
    row = lambda i, f: (i, 0)
    return pl.pallas_call(
        _ffn_kernel,
        grid=(N_TOK // tm, D_FF // tf),
        in_specs=[
            pl.BlockSpec((tm, D_MODEL), row),
            pl.BlockSpec((D_MODEL, tf), lambda i, f: (0, f)),
            pl.BlockSpec((D_MODEL, tf), lambda i, f: (0, f)),
            pl.BlockSpec((tf, D_MODEL), lambda i, f: (f, 0)),
            pl.BlockSpec((tm, D_MODEL), row),
        ],
        out_specs=pl.BlockSpec((tm, D_MODEL), row),
        out_shape=jax.ShapeDtypeStruct((N_TOK, D_MODEL), F32),
        scratch_shapes=[pltpu.VMEM((tm, D_MODEL), F32)],
        compiler_params=_cparams(("parallel", "arbitrary"), 56),
        name="ffn",
    )(h, wg, wu, wd, x)


def _final_norm_kernel(x_ref, g_ref, o_ref):
    o_ref[...] = _rms(x_ref[...], g_ref[...])


def _final_norm(x, g):
    tm = TM_NORM
    return pl.pallas_call(
        _final_norm_kernel,
        grid=(N_TOK // tm,),
        in_specs=[pl.BlockSpec((tm, D_MODEL), lambda i: (i, 0)),
                  pl.BlockSpec((1, D_MODEL), lambda i: (0, 0))],
        out_specs=pl.BlockSpec((tm, D_MODEL), lambda i: (i, 0)),
        out_shape=jax.ShapeDtypeStruct((N_TOK, D_MODEL), F32),
        compiler_params=_cparams(("parallel",), 32),
        name="final_norm",
    )(x, g)


def _pad_cols(w, width):
    return jnp.pad(w, ((0, 0), (0, width - w.shape[1])))


def _swap_halves(w):
    half = QK_ROPE // 2
    return jnp.concatenate([w[..., half:], w[..., :half]], axis=-1)


def _prep_w_in(w):
    kr = w[:, OFF_KR:OFF_GLU]
    return jnp.concatenate([
        w[:, OFF_POOL:OFF_KR],
        _pad_cols(kr, LANES), _pad_cols(_swap_halves(kr), LANES),
        w[:, OFF_GLU:]], axis=1).astype(BF16)


def _prep_w_uq(w_uq):
    def per_head(part):
        pad = LANES - part.shape[-1]
        return jnp.pad(part, ((0, 0), (0, 0), (0, pad))).reshape(Q_LORA, N_HEADS * LANES).astype(BF16)

    rope = w_uq[..., QK_NOPE:]
    return per_head(w_uq[..., :QK_NOPE]), per_head(rope), per_head(_swap_halves(rope))


def _prep_w_uk(w_uk):
    wt = jnp.transpose(w_uk, (1, 2, 0))
    return jnp.pad(wt, ((0, 0), (0, LANES - QK_NOPE), (0, 0))).astype(BF16)


def _prep_w_uv(w_uv):
    eye = jnp.eye(N_HEADS, dtype=w_uv.dtype)
    wbd = jnp.einsum('rhd,hg->hrgd', w_uv, eye)
    return wbd.reshape(LAT_COLS, ATTN_WIDTH).astype(BF16)


def _prep_w_pool(w_pool):
    n = len(POOL_WINDOWS)
    eye = jnp.eye(n, dtype=w_pool.dtype)
    return jnp.einsum('gcd,gk->gckd', w_pool, eye).reshape(POOL_WIDTH, POOL_WIDTH).astype(BF16)


def _rope_tables():
    inv = ROPE_THETA ** (-jnp.arange(0, QK_ROPE, 2, dtype=F32) / QK_ROPE)
    pos = jnp.concatenate([jnp.tile(jnp.arange(SEQ), BATCH),
                           jnp.tile(PAST_LEN + jnp.arange(DEC_SEQ), DEC_BATCH)])
    ang = pos.astype(F32)[:, None] * inv[None, :]
    cos, sin = jnp.cos(ang), jnp.sin(ang)
    cos_t = _pad_cols(jnp.concatenate([cos, cos], axis=1), LANES)
    sin_t = _pad_cols(jnp.concatenate([-sin, sin], axis=1), LANES)
    return cos_t, sin_t


def _split_hi_lo(w):
    hi = w.astype(BF16)
    lo = (w - hi.astype(F32)).astype(BF16)
    return hi, lo


def kernel(x_prompt, x_sample, cache_ckv, cache_krope, page_table, state_pool, state_conv, g_mix_norm, w_in, b_gate, w_pool, s_pool, g_q_lat, w_uq, g_kv_lat, w_uk, w_uv, w_dw, b_dw, g_conv_ln, b_conv_ln, w_br, w_out, g_ffn_norm, w_d_gate, w_d_up, w_d_down, w_router, w_e_gate, w_e_up, w_e_down, g_final):
    x = jnp.concatenate([x_prompt.reshape(N_PROMPT, D_MODEL),
                         x_sample.reshape(N_SAMPLE, D_MODEL)], axis=0)
    cos_t, sin_t = _rope_tables()
    pt_flat = page_table.reshape(-1)
    cache_krope_t = jnp.swapaxes(cache_krope, 2, 3)
    expert_w = (w_e_gate.astype(BF16), w_e_up.astype(BF16), w_e_down.astype(BF16))

    ckv_p, kr_p, pool_p, conv_p = [], [], [], []
    ckv_s, kr_s, pool_s, conv_s = [], [], [], []
    for l in range(DEPTH):
        u_pool, c_q, kvb, c_kv, k_r, z, gates = _inproj(
            x, g_mix_norm[l][None], _prep_w_in(w_in[l]), b_gate[l][None],
            g_q_lat[l][None], g_kv_lat[l][None], cos_t, sin_t)

        q_args = (c_q, *_prep_w_uq(w_uq[l]), _prep_w_uk(w_uk[l]), cos_t, sin_t)
        qt_p = _qproj(True, *q_args)
        q_s = _qproj(False, *q_args)

        wbd = _prep_w_pool(w_pool[l])
        sp = s_pool[l][None]
        u_p = u_pool[:N_PROMPT].reshape(BATCH, SEQ, POOL_WIDTH)
        u_s = u_pool[N_PROMPT:].reshape(DEC_BATCH, DEC_SEQ, POOL_WIDTH)
        pool_ext = jnp.concatenate(
            [jnp.zeros((DEC_BATCH, POOL_HALO - POOL_BUF, POOL_WIDTH), F32), state_pool[l], u_s], axis=1)
        o_pool = (_pool_prompt(u_p, wbd, sp).reshape(N_PROMPT, POOL_WIDTH),
                  _pool_sample(pool_ext, wbd, sp).reshape(N_SAMPLE, POOL_WIDTH))

        z_p = z[:N_PROMPT].reshape(BATCH, SEQ, CONV_WIDTH)
        z_s = z[N_PROMPT:].reshape(DEC_BATCH, DEC_SEQ, CONV_WIDTH)
        conv_ext = jnp.concatenate(
            [jnp.zeros((DEC_BATCH, CONV_HALO - CONV_BUF, CONV_WIDTH), F32), state_conv[l], z_s], axis=1)
        conv_args = (w_dw[l], b_dw[l][None], g_conv_ln[l][None], b_conv_ln[l][None])
        o_conv = (_conv_prompt(z_p, *conv_args).reshape(N_PROMPT, CONV_WIDTH),
                  _conv_sample(conv_ext, *conv_args).reshape(N_SAMPLE, CONV_WIDTH))

        nblk = SEQ // TQ
        kv_p = kvb[:N_PROMPT].reshape(BATCH, nblk, TQ, Q_HEAD)
        vt_p = jnp.concatenate([jnp.swapaxes(kv_p[..., :KV_LORA], 2, 3),
                                jnp.ones((BATCH, nblk, Q_HEAD - KV_LORA, TQ), BF16)], axis=2)
        lat_p = _attn_prompt(qt_p, kv_p, vt_p)
        q_s = q_s.reshape(DEC_BATCH, DEC_SEQ, N_HEADS, Q_HEAD)
        q_s = jnp.swapaxes(q_s, 1, 2).reshape(DEC_BATCH, N_HEADS * DEC_SEQ, Q_HEAD)
        kv_s = kvb[N_PROMPT:].reshape(DEC_BATCH, DEC_SEQ, Q_HEAD)
        kc_new = jnp.pad(kv_s[..., :KV_LORA], ((0, 0), (0, LANES - DEC_SEQ), (0, 0)))
        krt_new = jnp.pad(jnp.swapaxes(kv_s[..., KV_LORA:KV_LORA + QK_ROPE], 1, 2),
                          ((0, 0), (0, 0), (0, LANES - DEC_SEQ)))
        lat_s = _attn_sample(l, pt_flat, q_s, kc_new, krt_new, cache_ckv, cache_krope_t)
        lat_s = jnp.swapaxes(lat_s.reshape(DEC_BATCH, N_HEADS, DEC_SEQ, KV_LORA), 1, 2)
        lat = (lat_p.reshape(N_PROMPT, LAT_COLS), lat_s.reshape(N_SAMPLE, LAT_COLS))

        is_moe = l % 2 == 1
        wr = _pad_cols(w_router[l // 2], LANES) if is_moe else jnp.zeros((D_MODEL, LANES), F32)
        wrh, wrl = _split_hi_lo(wr)
        x, h, route = _merge(is_moe, o_pool, lat, o_conv, gates, x, _prep_w_uv(w_uv[l]),
                             w_br[l].astype(BF16), w_out[l].astype(BF16), g_ffn_norm[l][None],
                             wrh, wrl)
        if is_moe:
            pos, tile_expert, n_active = _route(route)
            xs = _dispatch(pos, h, jnp.zeros((M_SORTED,) + ROW_TILE, F32))
            ys = _expert_ffn(l // 2, tile_expert, n_active, xs, *expert_w)
            x = _combine(pos, ys, x, route)
        else:
            x = _ffn(h, w_d_gate[l // 2].astype(BF16), w_d_up[l // 2].astype(BF16),
                     w_d_down[l // 2].astype(BF16), x)

        ckv_p.append(c_kv[:N_PROMPT].reshape(BATCH, SEQ, KV_LORA))
        kr_p.append(k_r[:N_PROMPT].reshape(BATCH, SEQ, QK_ROPE))
        pool_p.append(u_p[:, -POOL_BUF:])
        conv_p.append(z_p[:, -CONV_BUF:])
        ckv_s.append(c_kv[N_PROMPT:].reshape(DEC_BATCH, DEC_SEQ, KV_LORA))
        kr_s.append(k_r[N_PROMPT:].reshape(DEC_BATCH, DEC_SEQ, QK_ROPE))
        pool_s.append(jnp.concatenate([state_pool[l], u_s], axis=1)[:, -POOL_BUF:])
        conv_s.append(jnp.concatenate([state_conv[l], z_s], axis=1)[:, -CONV_BUF:])

    y = _final_norm(x, g_final[None])
    return (y[:N_PROMPT].reshape(BATCH, SEQ, D_MODEL),
            y[N_PROMPT:].reshape(DEC_BATCH, DEC_SEQ, D_MODEL),
            jnp.stack(ckv_p), jnp.stack(kr_p), jnp.stack(pool_p), jnp.stack(conv_p),
            jnp.stack(ckv_s), jnp.stack(kr_s), jnp.stack(pool_s), jnp.stack(conv_s))
```

```python
import functools

import jax
import jax.numpy as jnp
from jax import lax
from jax.experimental import pallas as pl
from jax.experimental.pallas import tpu as pltpu

F32 = jnp.float32
BF16 = jnp.bfloat16

D_MODEL = 1024
BATCH = 8
SEQ = 2048
DEPTH = 4
DEC_BATCH = 128
DEC_SEQ = 8
PAST_LEN = 8192
PAGE_SIZE = 128
N_PAGES = PAST_LEN // PAGE_SIZE

POOL_WINDOWS = (2, 4, 8, 16)
POOL_GROUP = 64
POOL_WIDTH = 256
POOL_BUF = 15

N_HEADS = 8
QK_NOPE = 64
QK_ROPE = 32
V_DIM = 64
Q_LORA = 256
KV_LORA = 128
ROPE_THETA = 10000.0
ATTN_WIDTH = N_HEADS * V_DIM
ATTN_SCALE = (QK_NOPE + QK_ROPE) ** -0.5

CONV_WIDTH = 256
CONV_K = 31
CONV_BUF = CONV_K - 1

N_BRANCH = 3
OFF_POOL = 0
OFF_Q = OFF_POOL + POOL_WIDTH
OFF_KV = OFF_Q + Q_LORA
OFF_KR = OFF_KV + KV_LORA
OFF_GLU = OFF_KR + QK_ROPE
OFF_GATE = OFF_GLU + 2 * CONV_WIDTH

D_FF = 2816
N_EXPERTS = 8
D_FF_EXPERT = 3584

EPS = 1e-6
NEG_INF = -1e30

N_PROMPT = BATCH * SEQ
N_SAMPLE = DEC_BATCH * DEC_SEQ
N_TOK = N_PROMPT + N_SAMPLE

LANES = 128
VMEM_BYTES_V7X = 64 * 1024 * 1024

C_POOL = 0
C_Q = C_POOL + POOL_WIDTH
C_KV = C_Q + Q_LORA
C_KRA = C_KV + KV_LORA
C_KRB = C_KRA + LANES
C_GLU = C_KRB + LANES
C_GATE = C_GLU + 2 * CONV_WIDTH
C_END = C_GATE + N_BRANCH * D_MODEL

Q_HEAD = 2 * LANES
Q_COLS = N_HEADS * Q_HEAD
LAT_COLS = N_HEADS * KV_LORA

TM_TOKEN = 512
TM_DENSE = 512
TF_DENSE = 1408
TM_EXPERT = 512
TF_EXPERT = 1792
TOP_K = 2
N_PAIRS = TOP_K * N_TOK
EXPERT_TILES = N_PAIRS // TM_EXPERT + N_EXPERTS
M_SORTED = EXPERT_TILES * TM_EXPERT
ROW_TILE = (8, LANES)
DMA_CHUNK = 1024
PAGE_BUFFERS = 4
TQ = 256
ATTN_ROWS = N_HEADS * TQ
LOG2_E = 1.4426950408889634
Q_SCALE = ATTN_SCALE * LOG2_E
POOL_HALO = 16
CONV_HALO = 32
SEQ_CHUNK = 256
CONV_ROWS = 64
SAMPLE_BB = 16


def _cparams(semantics, vmem_mib):
    assert vmem_mib * 1024 * 1024 < VMEM_BYTES_V7X
    return pltpu.CompilerParams(dimension_semantics=semantics,
                                vmem_limit_bytes=vmem_mib * 1024 * 1024)


def _rms(x, g):
    return x * lax.rsqrt(jnp.mean(x * x, axis=-1, keepdims=True) + EPS) * g


def _dot(a, b):
    return jnp.dot(a, b, preferred_element_type=F32)


def _dot_nt(a, b):
    return lax.dot_general(a, b, (((1,), (1,)), ((), ())), preferred_element_type=F32)


def _inproj_kernel(x_ref, g_ref, w_ref, bg_ref, gq_ref, gkv_ref, cos_ref, sin_ref,
                   up_ref, cq_ref, kvb_ref, ckv_ref, kr_ref, z_ref, gate_ref):
    h = _rms(x_ref[...], g_ref[...]).astype(BF16)

    def proj(lo, hi):
        return _dot(h, w_ref[:, lo:hi])

    up_ref[...] = proj(C_POOL, C_Q)
    cq_ref[...] = _rms(proj(C_Q, C_KV), gq_ref[...]).astype(BF16)
    ckv = _rms(proj(C_KV, C_KRA), gkv_ref[...])
    kr = proj(C_KRA, C_KRB) * cos_ref[...] + proj(C_KRB, C_GLU) * sin_ref[...]
    ckv_ref[...] = ckv
    kr_ref[...] = kr[:, :QK_ROPE]
    kvb_ref[:, :KV_LORA] = ckv.astype(BF16)
    kvb_ref[:, KV_LORA:] = kr.astype(BF16)
    glu = proj(C_GLU, C_GATE)
    z_ref[...] = glu[:, :CONV_WIDTH] * jax.nn.sigmoid(glu[:, CONV_WIDTH:])
    for c in range(N_BRANCH):
        lo = c * D_MODEL
        g = proj(C_GATE + lo, C_GATE + lo + D_MODEL) + bg_ref[:, lo:lo + D_MODEL]
        gate_ref[:, lo:lo + D_MODEL] = jax.nn.sigmoid(g).astype(BF16)


def _inproj(x, g, w, bg, gq, gkv, cos, sin):
    tm = TM_TOKEN
    row = lambda i: (i, 0)
    fix = lambda i: (0, 0)
    return pl.pallas_call(
        _inproj_kernel,
        grid=(N_TOK // tm,),
        in_specs=[
            pl.BlockSpec((tm, D_MODEL), row),
            pl.BlockSpec((1, D_MODEL), fix),
            pl.BlockSpec((D_MODEL, C_END), fix),
            pl.BlockSpec((1, N_BRANCH * D_MODEL), fix),
            pl.BlockSpec((1, Q_LORA), fix),
            pl.BlockSpec((1, KV_LORA), fix),
            pl.BlockSpec((tm, LANES), row),
            pl.BlockSpec((tm, LANES), row),
        ],
        out_specs=[
            pl.BlockSpec((tm, POOL_WIDTH), row),
            pl.BlockSpec((tm, Q_LORA), row),
            pl.BlockSpec((tm, Q_HEAD), row),
            pl.BlockSpec((tm, KV_LORA), row),
            pl.BlockSpec((tm, QK_ROPE), row),
            pl.BlockSpec((tm, CONV_WIDTH), row),
            pl.BlockSpec((tm, N_BRANCH * D_MODEL), row),
        ],
        out_shape=[
            jax.ShapeDtypeStruct((N_TOK, POOL_WIDTH), F32),
            jax.ShapeDtypeStruct((N_TOK, Q_LORA), BF16),
            jax.ShapeDtypeStruct((N_TOK, Q_HEAD), BF16),
            jax.ShapeDtypeStruct((N_TOK, KV_LORA), F32),
            jax.ShapeDtypeStruct((N_TOK, QK_ROPE), F32),
            jax.ShapeDtypeStruct((N_TOK, CONV_WIDTH), F32),
            jax.ShapeDtypeStruct((N_TOK, N_BRANCH * D_MODEL), BF16),
        ],
        compiler_params=_cparams(("parallel",), 56),
        name="inproj",
    )(x, g, w, bg, gq, gkv, cos, sin)


def _q_heads(cq_ref, wn_ref, wa_ref, wb_ref, wuk_ref, cos_ref, sin_ref):
    cq = cq_ref[...]
    qn = _dot(cq, wn_ref[...]).astype(BF16)
    ra = _dot(cq, wa_ref[...])
    rb = _dot(cq, wb_ref[...])
    cos = cos_ref[...]
    sin = sin_ref[...]
    for h in range(N_HEADS):
        sl = slice(h * LANES, (h + 1) * LANES)
        yield (h, _dot(qn[:, sl], wuk_ref[h]) * Q_SCALE,
               (ra[:, sl] * cos + rb[:, sl] * sin) * Q_SCALE)


def _qproj_sample_kernel(*refs):
    q_ref = refs[-1]
    for h, q_abs, q_rope in _q_heads(*refs[:-1]):
        q_ref[:, h * Q_HEAD:h * Q_HEAD + LANES] = q_abs.astype(BF16)
        q_ref[:, h * Q_HEAD + LANES:(h + 1) * Q_HEAD] = q_rope.astype(BF16)


def _qproj_prompt_kernel(*refs):
    qt_ref = refs[-1]
    for h, q_abs, q_rope in _q_heads(*refs[:-1]):
        for c in range(TM_TOKEN // TQ):
            rows = slice(c * TQ, (c + 1) * TQ)
            cols = slice(h * TQ, (h + 1) * TQ)
            qt_ref[0, c, :LANES, cols] = q_abs[rows].T.astype(BF16)
            qt_ref[0, c, LANES:, cols] = q_rope[rows].T.astype(BF16)


def _qproj(for_prompt, cq, wn, wa, wb, wuk, cos, sin):
    tm = TM_TOKEN
    fix = lambda i: (0, 0)
    if for_prompt:
        n_steps, first = N_PROMPT // tm, 0
        tiles = tm // TQ
        per_seq = SEQ // tm
        out_spec = pl.BlockSpec((1, tiles, Q_HEAD, ATTN_ROWS),
                                lambda i: (i // per_seq, i % per_seq, 0, 0))
        out_shape = jax.ShapeDtypeStruct((BATCH, SEQ // TQ, Q_HEAD, ATTN_ROWS), BF16)
    else:
        n_steps, first = N_SAMPLE // tm, N_PROMPT // tm
        out_spec = pl.BlockSpec((tm, Q_COLS), lambda i: (i, 0))
        out_shape = jax.ShapeDtypeStruct((N_SAMPLE, Q_COLS), BF16)
    row = lambda i: (first + i, 0)
    return pl.pallas_call(
        _qproj_prompt_kernel if for_prompt else _qproj_sample_kernel,
        grid=(n_steps,),
        in_specs=[
            pl.BlockSpec((tm, Q_LORA), row),
            pl.BlockSpec((Q_LORA, N_HEADS * LANES), fix),
            pl.BlockSpec((Q_LORA, N_HEADS * LANES), fix),
            pl.BlockSpec((Q_LORA, N_HEADS * LANES), fix),
            pl.BlockSpec((N_HEADS, LANES, KV_LORA), lambda i: (0, 0, 0)),
            pl.BlockSpec((tm, LANES), row),
            pl.BlockSpec((tm, LANES), row),
        ],
        out_specs=out_spec,
        out_shape=out_shape,
        compiler_params=_cparams(("parallel",), 40),
        name="qproj_prompt" if for_prompt else "qproj_sample",
    )(cq, wn, wa, wb, wuk, cos, sin)


def _pool_compute(ext_ref, n_rows, cnt_of_window, wbd_ref, sp_ref):
    def ld(j):
        return ext_ref[:, pl.ds(POOL_HALO - j, n_rows), :]

    tok = ld(0)
    run = tok
    sums = {}
    for j in range(1, max(POOL_WINDOWS)):
        run = run + ld(j)
        if j + 1 in POOL_WINDOWS:
            sums[j + 1] = run
    lane = lax.broadcasted_iota(jnp.int32, tok.shape, 2)
    pooled = sums[POOL_WINDOWS[-1]] / cnt_of_window(POOL_WINDOWS[-1])
    for g in range(len(POOL_WINDOWS) - 2, -1, -1):
        w = POOL_WINDOWS[g]
        pooled = jnp.where(lane < (g + 1) * POOL_GROUP, sums[w] / cnt_of_window(w), pooled)
    pooled = (pooled - tok).reshape(-1, POOL_WIDTH).astype(BF16)
    return _dot(pooled, wbd_ref[...]) * sp_ref[...]


def _pool_prompt_kernel(halo_ref, u_ref, wbd_ref, sp_ref, o_ref, ext_ref):
    i = pl.program_id(1)
    ext_ref[:, :POOL_HALO, :] = jnp.where(i > 0, halo_ref[...], 0.0)
    ext_ref[:, POOL_HALO:, :] = u_ref[...]
    pos = lax.broadcasted_iota(jnp.int32, (1, SEQ_CHUNK, 1), 1) + i * SEQ_CHUNK

    def cnt(w):
        return jnp.minimum(pos + 1, w).astype(F32)

    o_ref[0] = _pool_compute(ext_ref, SEQ_CHUNK, cnt, wbd_ref, sp_ref).astype(BF16)


def _pool_prompt(u, wbd, sp):
    hb = SEQ_CHUNK // POOL_HALO
    return pl.pallas_call(
        _pool_prompt_kernel,
        grid=(BATCH, SEQ // SEQ_CHUNK),
        in_specs=[
            pl.BlockSpec((1, POOL_HALO, POOL_WIDTH),
                         lambda b, i: (b, jnp.maximum(i * hb - 1, 0), 0)),
            pl.BlockSpec((1, SEQ_CHUNK, POOL_WIDTH), lambda b, i: (b, i, 0)),
            pl.BlockSpec((POOL_WIDTH, POOL_WIDTH), lambda b, i: (0, 0)),
            pl.BlockSpec((1, POOL_WIDTH), lambda b, i: (0, 0)),
        ],
        out_specs=pl.BlockSpec((1, SEQ_CHUNK, POOL_WIDTH), lambda b, i: (b, i, 0)),
        out_shape=jax.ShapeDtypeStruct((BATCH, SEQ, POOL_WIDTH), BF16),
        scratch_shapes=[pltpu.VMEM((1, POOL_HALO + SEQ_CHUNK, POOL_WIDTH), F32)],
        compiler_params=_cparams(("parallel", "parallel"), 32),
        name="pool_prompt",
    )(u, u, wbd, sp)


def _pool_sample_kernel(ext_ref, wbd_ref, sp_ref, o_ref):
    out = _pool_compute(ext_ref, DEC_SEQ, lambda w: float(w), wbd_ref, sp_ref)
    o_ref[...] = out.reshape(SAMPLE_BB, DEC_SEQ, POOL_WIDTH).astype(BF16)


def _pool_sample(ext, wbd, sp):
    return pl.pallas_call(
        _pool_sample_kernel,
        grid=(DEC_BATCH // SAMPLE_BB,),
        in_specs=[
            pl.BlockSpec((SAMPLE_BB, POOL_HALO + DEC_SEQ, POOL_WIDTH), lambda i: (i, 0, 0)),
            pl.BlockSpec((POOL_WIDTH, POOL_WIDTH), lambda i: (0, 0)),
            pl.BlockSpec((1, POOL_WIDTH), lambda i: (0, 0)),
        ],
        out_specs=pl.BlockSpec((SAMPLE_BB, DEC_SEQ, POOL_WIDTH), lambda i: (i, 0, 0)),
        out_shape=jax.ShapeDtypeStruct((DEC_BATCH, DEC_SEQ, POOL_WIDTH), BF16),
        compiler_params=_cparams(("parallel",), 32),
        name="pool_sample",
    )(ext, wbd, sp)


def _conv_compute(ext_ref, row0, n_rows, wdw_ref, bdw_ref, g_ref, b_ref):
    lead = CONV_HALO - CONV_BUF
    acc = None
    for k in range(CONV_K):
        term = ext_ref[:, pl.ds(row0 + lead + k, n_rows), :] * wdw_ref[k:k + 1, :]
        acc = term if acc is None else acc + term
    zc = acc + bdw_ref[...]
    mu = jnp.mean(zc, axis=-1, keepdims=True)
    xc = zc - mu
    y = xc * lax.rsqrt(jnp.mean(xc * xc, axis=-1, keepdims=True) + EPS)
    y = y * g_ref[...] + b_ref[...]
    return y * jax.nn.sigmoid(y)


def _conv_prompt_kernel(halo_ref, z_ref, wdw_ref, bdw_ref, g_ref, b_ref, o_ref, ext_ref):
    i = pl.program_id(1)
    ext_ref[:, :CONV_HALO, :] = jnp.where(i > 0, halo_ref[...], 0.0)
    ext_ref[:, CONV_HALO:, :] = z_ref[...]
    for r0 in range(0, SEQ_CHUNK, CONV_ROWS):
        o_ref[:, r0:r0 + CONV_ROWS, :] = _conv_compute(
            ext_ref, r0, CONV_ROWS, wdw_ref, bdw_ref, g_ref, b_ref).astype(BF16)


def _conv_prompt(z, wdw, bdw, g, b):
    hb = SEQ_CHUNK // CONV_HALO
    vec = pl.BlockSpec((1, CONV_WIDTH), lambda bb, i: (0, 0))
    return pl.pallas_call(
        _conv_prompt_kernel,
        grid=(BATCH, SEQ // SEQ_CHUNK),
        in_specs=[
            pl.BlockSpec((1, CONV_HALO, CONV_WIDTH),
                         lambda bb, i: (bb, jnp.maximum(i * hb - 1, 0), 0)),
            pl.BlockSpec((1, SEQ_CHUNK, CONV_WIDTH), lambda bb, i: (bb, i, 0)),
            pl.BlockSpec((CONV_K, CONV_WIDTH), lambda bb, i: (0, 0)),
            vec, vec, vec,
        ],
        out_specs=pl.BlockSpec((1, SEQ_CHUNK, CONV_WIDTH), lambda bb, i: (bb, i, 0)),
        out_shape=jax.ShapeDtypeStruct((BATCH, SEQ, CONV_WIDTH), BF16),
        scratch_shapes=[pltpu.VMEM((1, CONV_HALO + SEQ_CHUNK, CONV_WIDTH), F32)],
        compiler_params=_cparams(("parallel", "parallel"), 32),
        name="conv_prompt",
    )(z, z, wdw, bdw, g, b)


def _conv_sample_kernel(ext_ref, wdw_ref, bdw_ref, g_ref, b_ref, o_ref):
    o_ref[...] = _conv_compute(ext_ref, 0, DEC_SEQ, wdw_ref, bdw_ref, g_ref, b_ref).astype(BF16)


def _conv_sample(ext, wdw, bdw, g, b):
    vec = pl.BlockSpec((1, CONV_WIDTH), lambda i: (0, 0))
    return pl.pallas_call(
        _conv_sample_kernel,
        grid=(DEC_BATCH // SAMPLE_BB,),
        in_specs=[
            pl.BlockSpec((SAMPLE_BB, CONV_HALO + DEC_SEQ, CONV_WIDTH), lambda i: (i, 0, 0)),
            pl.BlockSpec((CONV_K, CONV_WIDTH), lambda i: (0, 0)),
            vec, vec, vec,
        ],
        out_specs=pl.BlockSpec((SAMPLE_BB, DEC_SEQ, CONV_WIDTH), lambda i: (i, 0, 0)),
        out_shape=jax.ShapeDtypeStruct((DEC_BATCH, DEC_SEQ, CONV_WIDTH), BF16),
        compiler_params=_cparams(("parallel",), 32),
        name="conv_sample",
    )(ext, wdw, bdw, g, b)


def _attn_prompt_kernel(qt_ref, k_ref, vt_ref, o_ref, m_ref, acc_ref):
    qi = pl.program_id(1)
    qt = qt_ref[0, 0]
    m_ref[...] = jnp.full(m_ref.shape, NEG_INF, F32)
    acc_ref[...] = jnp.zeros(acc_ref.shape, F32)

    def block(j, on_diagonal):
        s = _dot(k_ref[0, j], qt)
        if on_diagonal:
            tok = lax.broadcasted_iota(jnp.int32, s.shape, 1) & (TQ - 1)
            key = lax.broadcasted_iota(jnp.int32, s.shape, 0)
            s = jnp.where(key <= tok, s, NEG_INF)
        m_old = m_ref[...]
        m_new = jnp.maximum(m_old, jnp.max(s, axis=0, keepdims=True))
        p = jnp.exp2(s - m_new).astype(BF16)
        acc_ref[...] = jnp.exp2(m_old - m_new) * acc_ref[...] + _dot(vt_ref[0, j], p)
        m_ref[...] = m_new

    def below_diagonal(j, carry):
        block(j, False)
        return carry

    lax.fori_loop(0, qi, below_diagonal, 0)
    block(qi, True)
    acc = acc_ref[...]
    out = (acc[:KV_LORA] / acc[KV_LORA:]).T.astype(BF16)
    for h in range(N_HEADS):
        o_ref[0, :, h * KV_LORA:(h + 1) * KV_LORA] = out[h * TQ:(h + 1) * TQ]


def _attn_prompt(qt, k, vt):
    nblk = SEQ // TQ
    return pl.pallas_call(
        _attn_prompt_kernel,
        grid=(BATCH, nblk),
        in_specs=[
            pl.BlockSpec((1, 1, Q_HEAD, ATTN_ROWS), lambda b, i: (b, i, 0, 0)),
            pl.BlockSpec((1, nblk, TQ, Q_HEAD), lambda b, i: (b, 0, 0, 0)),
            pl.BlockSpec((1, nblk, Q_HEAD, TQ), lambda b, i: (b, 0, 0, 0)),
        ],
        out_specs=pl.BlockSpec((1, TQ, LAT_COLS), lambda b, i: (b, i, 0)),
        out_shape=jax.ShapeDtypeStruct((BATCH, SEQ, LAT_COLS), BF16),
        scratch_shapes=[pltpu.VMEM((1, ATTN_ROWS), F32), pltpu.VMEM((Q_HEAD, ATTN_ROWS), F32)],
        compiler_params=_cparams(("parallel", "arbitrary"), 40),
        name="attn_prompt",
    )(qt, k, vt)


def _attn_sample_kernel(layer, pt_ref, q_ref, kcn_ref, krn_ref, ckv_hbm, krt_hbm, o_ref, *scratch):
    b = pl.program_id(0)
    n = pl.num_programs(0)
    ckbufs = scratch[:PAGE_BUFFERS]
    krbufs = scratch[PAGE_BUFFERS:2 * PAGE_BUFFERS]
    semc, semr = scratch[2 * PAGE_BUFFERS:]
    ahead = PAGE_BUFFERS - 1

    def issue(seq, slot):
        for p in range(N_PAGES):
            pid = pt_ref[seq * N_PAGES + p]
            pltpu.make_async_copy(ckv_hbm.at[layer, pid], ckbufs[slot].at[p],
                                  semc.at[slot]).start(priority=p % 2)
            pltpu.make_async_copy(krt_hbm.at[layer, pid], krbufs[slot].at[p],
                                  semr.at[slot]).start(priority=p % 2)

    def drain(slot):
        pltpu.make_async_copy(ckv_hbm.at[layer, pl.ds(0, N_PAGES)], ckbufs[slot], semc.at[slot]).wait()
        pltpu.make_async_copy(krt_hbm.at[layer, pl.ds(0, N_PAGES)], krbufs[slot], semr.at[slot]).wait()

    @pl.when(b == 0)
    def _():
        for first in range(ahead):
            issue(first, first)

    def update(state, s, kc):
        m, l, acc = state
        m_new = jnp.maximum(m, jnp.max(s, axis=-1, keepdims=True))
        alpha = jnp.exp2(m - m_new)
        p = jnp.exp2(s - m_new)
        return (m_new, alpha * l + jnp.sum(p, axis=-1, keepdims=True),
                alpha * acc + _dot(p.astype(BF16), kc))

    def step(slot):
        drain(slot)
        issue(jnp.minimum(b + ahead, n - 1), (slot + ahead) % PAGE_BUFFERS)
        q = q_ref[0]
        qa = q[:, :KV_LORA]
        qr = q[:, KV_LORA:KV_LORA + QK_ROPE]
        rows = N_HEADS * DEC_SEQ
        state = (jnp.full((rows, 1), NEG_INF, F32), jnp.zeros((rows, 1), F32),
                 jnp.zeros((rows, KV_LORA), F32))
        kc = ckbufs[slot][...].reshape(PAST_LEN, KV_LORA).astype(BF16)
        kr = jnp.concatenate([krbufs[slot][p] for p in range(N_PAGES)], axis=-1).astype(BF16)
        state = update(state, _dot_nt(qa, kc) + _dot(qr, kr), kc)
        kcn = kcn_ref[0]
        s = _dot_nt(qa, kcn) + _dot(qr, krn_ref[0])
        t = lax.broadcasted_iota(jnp.int32, s.shape, 0) & (DEC_SEQ - 1)
        key = lax.broadcasted_iota(jnp.int32, s.shape, 1)
        _, l, acc = update(state, jnp.where(key <= t, s, NEG_INF), kcn)
        o_ref[0] = (acc / l).astype(BF16)

        @pl.when(b == n - 1)
        def _():
            for later in range(1, PAGE_BUFFERS):
                drain((slot + later) % PAGE_BUFFERS)

    for slot in range(PAGE_BUFFERS):
        pl.when(b % PAGE_BUFFERS == slot)(functools.partial(step, slot))


def _attn_sample(layer, page_table, q, kc_new, krt_new, cache_ckv, cache_krope_t):
    page_ck = pltpu.VMEM((N_PAGES, PAGE_SIZE, KV_LORA), F32)
    page_kr = pltpu.VMEM((N_PAGES, QK_ROPE, PAGE_SIZE), F32)
    grid_spec = pltpu.PrefetchScalarGridSpec(
        num_scalar_prefetch=1,
        grid=(DEC_BATCH,),
        in_specs=[pl.BlockSpec((1, N_HEADS * DEC_SEQ, Q_HEAD), lambda b, pt: (b, 0, 0)),
                  pl.BlockSpec((1, LANES, KV_LORA), lambda b, pt: (b, 0, 0)),
                  pl.BlockSpec((1, QK_ROPE, LANES), lambda b, pt: (b, 0, 0)),
                  pl.BlockSpec(memory_space=pl.ANY),
                  pl.BlockSpec(memory_space=pl.ANY)],
        out_specs=pl.BlockSpec((1, N_HEADS * DEC_SEQ, KV_LORA), lambda b, pt: (b, 0, 0)),
        scratch_shapes=([page_ck] * PAGE_BUFFERS + [page_kr] * PAGE_BUFFERS
                        + [pltpu.SemaphoreType.DMA((PAGE_BUFFERS,))] * 2),
    )
    return pl.pallas_call(
        functools.partial(_attn_sample_kernel, layer),
        grid_spec=grid_spec,
        out_shape=jax.ShapeDtypeStruct((DEC_BATCH, N_HEADS * DEC_SEQ, KV_LORA), BF16),
        compiler_params=_cparams(("arbitrary",), 48),
        name="attn_sample",
    )(page_table, q, kc_new, krt_new, cache_ckv, cache_krope_t)


def _merge_kernel(with_router, op_p, op_s, lat_p, lat_s, oc_p, oc_s, gate_ref, x_ref, wuv_ref,
                  wbr_ref, wout_ref, gf_ref, wrh_ref, wrl_ref, xo_ref, h_ref, route_ref):
    in_prompt = pl.program_id(0) < N_PROMPT // TM_TOKEN

    def pick(prompt_ref, sample_ref):
        return jnp.where(in_prompt, prompt_ref[...], sample_ref[...])

    o_attn = _dot(pick(lat_p, lat_s), wuv_ref[...]).astype(BF16)
    br_a = _dot(pick(op_p, op_s), wbr_ref[:POOL_WIDTH, :])
    br_b = _dot(o_attn, wbr_ref[POOL_WIDTH:POOL_WIDTH + ATTN_WIDTH, :])
    br_c = _dot(pick(oc_p, oc_s), wbr_ref[POOL_WIDTH + ATTN_WIDTH:, :])
    merged = (gate_ref[:, :D_MODEL].astype(F32) * br_a
              + gate_ref[:, D_MODEL:2 * D_MODEL].astype(F32) * br_b
              + gate_ref[:, 2 * D_MODEL:].astype(F32) * br_c)
    xn = x_ref[...] + _dot(merged.astype(BF16), wout_ref[...])
    xo_ref[...] = xn
    hn = _rms(xn, gf_ref[...])
    if not with_router:
        h_ref[...] = hn.astype(BF16)
        route_ref[...] = jnp.zeros(route_ref.shape, F32)
        return
    for c in range(ROW_TILE[0]):
        h_ref[:, c, :] = hn[:, c * LANES:(c + 1) * LANES]
    hh = hn.astype(BF16)
    hl = (hn - hh.astype(F32)).astype(BF16)
    lg = _dot(hh, wrh_ref[...]) + _dot(hl, wrh_ref[...]) + _dot(hh, wrl_ref[...])
    lane = lax.broadcasted_iota(jnp.int32, lg.shape, 1).astype(F32)
    lg = jnp.where(lane < N_EXPERTS, lg, NEG_INF)
    m1 = jnp.max(lg, axis=-1, keepdims=True)
    i1 = jnp.min(jnp.where(lg == m1, lane, float(LANES)), axis=-1, keepdims=True)
    lg2 = jnp.where(lane == i1, NEG_INF, lg)
    m2 = jnp.max(lg2, axis=-1, keepdims=True)
    i2 = jnp.min(jnp.where(lg2 == m2, lane, float(LANES)), axis=-1, keepdims=True)
    e = jnp.exp(m2 - m1)
    w1 = 1.0 / (1.0 + e)
    w2 = e / (1.0 + e)
    route_ref[...] = jnp.where(lane == 0, i1, jnp.where(lane == 1, i2, jnp.where(
        lane == 2, w1, jnp.where(lane == 3, w2, 0.0))))


def _merge(with_router, o_pool, lat, o_conv, gates, x, wuv, wbr, wout, gf, wrh, wrl):
    tm = TM_TOKEN
    row = lambda i: (i, 0)
    fix = lambda i: (0, 0)
    prompt_tiles = N_PROMPT // tm
    prompt_row = lambda i: (jnp.minimum(i, prompt_tiles - 1), 0)
    sample_row = lambda i: (jnp.maximum(i - prompt_tiles, 0), 0)

    def pair(width):
        return [pl.BlockSpec((tm, width), prompt_row), pl.BlockSpec((tm, width), sample_row)]
    if with_router:
        h_spec = pl.BlockSpec((tm,) + ROW_TILE, lambda i: (i, 0, 0))
        h_shape = jax.ShapeDtypeStruct((N_TOK,) + ROW_TILE, F32)
    else:
        h_spec = pl.BlockSpec((tm, D_MODEL), row)
        h_shape = jax.ShapeDtypeStruct((N_TOK, D_MODEL), BF16)
    return pl.pallas_call(
        functools.partial(_merge_kernel, with_router),
        grid=(N_TOK // tm,),
        in_specs=pair(POOL_WIDTH) + pair(LAT_COLS) + pair(CONV_WIDTH) + [
            pl.BlockSpec((tm, N_BRANCH * D_MODEL), row),
            pl.BlockSpec((tm, D_MODEL), row),
            pl.BlockSpec((LAT_COLS, ATTN_WIDTH), fix),
            pl.BlockSpec((D_MODEL, D_MODEL), fix),
            pl.BlockSpec((D_MODEL, D_MODEL), fix),
            pl.BlockSpec((1, D_MODEL), fix),
            pl.BlockSpec((D_MODEL, LANES), fix),
            pl.BlockSpec((D_MODEL, LANES), fix),
        ],
        out_specs=[
            pl.BlockSpec((tm, D_MODEL), row),
            h_spec,
            pl.BlockSpec((tm, LANES), row),
        ],
        out_shape=[
            jax.ShapeDtypeStruct((N_TOK, D_MODEL), F32),
            h_shape,
            jax.ShapeDtypeStruct((N_TOK, LANES), F32),
        ],
        compiler_params=_cparams(("parallel",), 48),
        name="merge_router" if with_router else "merge",
    )(*o_pool, *lat, *o_conv, gates, x, wuv, wbr, wout, gf, wrh, wrl)


def _row_chunk_copy(src_ref, dst_ref, sem):
    return pltpu.make_async_copy(src_ref.at[pl.ds(0, DMA_CHUNK)], dst_ref.at[pl.ds(0, DMA_CHUNK)], sem)


def _dispatch_kernel(pos_ref, h_ref, init_ref, xs_ref, sem):
    del init_ref
    pair0 = pl.program_id(0) * DMA_CHUNK

    def issue(r, carry):
        pltpu.make_async_copy(h_ref.at[r], xs_ref.at[pos_ref[pair0 + r]], sem).start()
        return carry

    lax.fori_loop(0, DMA_CHUNK, issue, 0, unroll=8)
    _row_chunk_copy(h_ref, xs_ref, sem).wait()


def _dispatch(pos, h_rows, init):
    token_chunks = N_TOK // DMA_CHUNK
    grid_spec = pltpu.PrefetchScalarGridSpec(
        num_scalar_prefetch=1,
        grid=(N_PAIRS // DMA_CHUNK,),
        in_specs=[pl.BlockSpec((DMA_CHUNK,) + ROW_TILE,
                               lambda i, pos: (i % token_chunks, 0, 0)),
                  pl.BlockSpec(memory_space=pl.ANY)],
        out_specs=pl.BlockSpec(memory_space=pl.ANY),
        scratch_shapes=[pltpu.SemaphoreType.DMA(())],
    )
    return pl.pallas_call(
        _dispatch_kernel,
        grid_spec=grid_spec,
        out_shape=jax.ShapeDtypeStruct((M_SORTED,) + ROW_TILE, F32),
        input_output_aliases={2: 0},
        compiler_params=_cparams(("arbitrary",), 24),
        name="dispatch",
    )(pos, h_rows, init)


def _expert_ffn_kernel(te_ref, na_ref, xs_ref, wg_ref, wu_ref, wd_ref, ys_ref, xb_ref, acc_ref):
    del te_ref
    i = pl.program_id(0)
    f = pl.program_id(1)
    active = i < na_ref[0]

    @pl.when(active & (f == 0))
    def _():
        xb_ref[...] = jnp.concatenate(
            [xs_ref[:, c, :] for c in range(ROW_TILE[0])], axis=-1).astype(BF16)

    @pl.when(active)
    def _():
        x = xb_ref[...]
        a = _dot(x, wg_ref[...])
        b = _dot(x, wu_ref[...])
        y = _dot((a * jax.nn.sigmoid(a) * b).astype(BF16), wd_ref[...])

        @pl.when(f == 0)
        def _():
            acc_ref[...] = y

        @pl.when(f > 0)
        def _():
            acc_ref[...] += y

    @pl.when(f == pl.num_programs(1) - 1)
    def _():
        y = jnp.where(active, acc_ref[...], 0.0)
        for c in range(ROW_TILE[0]):
            ys_ref[:, c, :] = y[:, c * LANES:(c + 1) * LANES]


def _expert_ffn(moe_layer, tile_expert, n_active, xs, wg, wu, wd):
    tm, tf = TM_EXPERT, TF_EXPERT
    nf = D_FF_EXPERT // tf

    def tile(i, na):
        return jnp.minimum(i, na[0] - 1)

    def hidden(i, f, na):
        return jnp.where(i < na[0], f, nf - 1)

    grid_spec = pltpu.PrefetchScalarGridSpec(
        num_scalar_prefetch=2,
        grid=(EXPERT_TILES, nf),
        in_specs=[
            pl.BlockSpec((tm,) + ROW_TILE, lambda i, f, te, na: (tile(i, na), 0, 0)),
            pl.BlockSpec((None, None, D_MODEL, tf),
                         lambda i, f, te, na: (moe_layer, te[tile(i, na)], 0, hidden(i, f, na))),
            pl.BlockSpec((None, None, D_MODEL, tf),
                         lambda i, f, te, na: (moe_layer, te[tile(i, na)], 0, hidden(i, f, na))),
            pl.BlockSpec((None, None, tf, D_MODEL),
                         lambda i, f, te, na: (moe_layer, te[tile(i, na)], hidden(i, f, na), 0)),
        ],
        out_specs=pl.BlockSpec((tm,) + ROW_TILE, lambda i, f, te, na: (i, 0, 0)),
        scratch_shapes=[pltpu.VMEM((tm, D_MODEL), BF16), pltpu.VMEM((tm, D_MODEL), F32)],
    )
    return pl.pallas_call(
        _expert_ffn_kernel,
        grid_spec=grid_spec,
        out_shape=jax.ShapeDtypeStruct((M_SORTED,) + ROW_TILE, F32),
        compiler_params=_cparams(("arbitrary", "arbitrary"), 56),
        name="expert_ffn",
    )(tile_expert, n_active, xs, wg, wu, wd)


def _combine_kernel(final_norm, pos_ref, ys_ref, x_ref, route_ref, g_ref, o_ref, buf_ref, sem):
    i = pl.program_id(0)
    n = pl.num_programs(0)
    tmc = DMA_CHUNK // TOP_K

    def issue(tile, slot):
        def body(r, carry):
            for k in range(TOP_K):
                pltpu.make_async_copy(ys_ref.at[pos_ref[k * N_TOK + tile * tmc + r]],
                                      buf_ref.at[slot, k * tmc + r], sem.at[slot]).start()
            return carry

        lax.fori_loop(0, tmc, body, 0, unroll=8)

    @pl.when(i == 0)
    def _():
        issue(0, 0)

    @pl.when(i + 1 < n)
    def _():
        issue(i + 1, (i + 1) % 2)

    slot = i % 2
    _row_chunk_copy(ys_ref, buf_ref.at[slot], sem.at[slot]).wait()
    w1 = route_ref[:, 2:3]
    w2 = route_ref[:, 3:4]
    for c in range(ROW_TILE[0]):
        cols = slice(c * LANES, (c + 1) * LANES)
        o_ref[:, cols] = x_ref[:, cols] + (w1 * buf_ref[slot, :tmc, c, :]
                                            + w2 * buf_ref[slot, tmc:, c, :])
    if final_norm:
        o_ref[...] = _rms(o_ref[...], g_ref[...])


def _combine(pos, ys, x, route, gain, final_norm):
    tmc = DMA_CHUNK // TOP_K
    grid_spec = pltpu.PrefetchScalarGridSpec(
        num_scalar_prefetch=1,
        grid=(N_TOK // tmc,),
        in_specs=[
            pl.BlockSpec(memory_space=pl.ANY),
            pl.BlockSpec((tmc, D_MODEL), lambda i, pos: (i, 0)),
            pl.BlockSpec((tmc, LANES), lambda i, pos: (i, 0)),
            pl.BlockSpec((1, D_MODEL), lambda i, pos: (0, 0)),
        ],
        out_specs=pl.BlockSpec((tmc, D_MODEL), lambda i, pos: (i, 0)),
        scratch_shapes=[pltpu.VMEM((2, DMA_CHUNK) + ROW_TILE, F32),
                        pltpu.SemaphoreType.DMA((2,))],
    )
    return pl.pallas_call(
        functools.partial(_combine_kernel, final_norm),
        grid_spec=grid_spec,
        out_shape=jax.ShapeDtypeStruct((N_TOK, D_MODEL), F32),
        compiler_params=_cparams(("arbitrary",), 40),
        name="combine_norm" if final_norm else "combine",
    )(pos, ys, x, route, gain)


def _route(route):
    experts = jnp.concatenate([route[:, 0], route[:, 1]]).astype(jnp.int32)
    one_hot = (experts[:, None] == jnp.arange(N_EXPERTS)[None, :]).astype(jnp.int32)
    running = jnp.cumsum(one_hot, axis=0)
    rank = jnp.sum(one_hot * running, axis=1) - 1
    tiles = (running[-1] + TM_EXPERT - 1) // TM_EXPERT
    tile_end = jnp.cumsum(tiles)
    pos = jnp.sum(one_hot * ((tile_end - tiles) * TM_EXPERT)[None, :], axis=1) + rank
    tile_ids = jnp.arange(EXPERT_TILES, dtype=jnp.int32)
    tile_expert = jnp.minimum(jnp.sum(tile_ids[:, None] >= tile_end[None, :], axis=1),
                              N_EXPERTS - 1)
    return pos.astype(jnp.int32), tile_expert.astype(jnp.int32), tile_end[-1:].astype(jnp.int32)


def _ffn_kernel(h_ref, wg_ref, wu_ref, wd_ref, x_ref, o_ref, acc_ref):
    f = pl.program_id(1)
    h = h_ref[...]
    a = _dot(h, wg_ref[...])
    b = _dot(h, wu_ref[...])
    y = _dot((a * jax.nn.sigmoid(a) * b).astype(BF16), wd_ref[...])

    @pl.when(f == 0)
    def _():
        acc_ref[...] = y

    @pl.when(f > 0)
    def _():
        acc_ref[...] += y

    @pl.when(f == pl.num_programs(1) - 1)
    def _():
        o_ref[...] = x_ref[...] + acc_ref[...]


def _ffn(h, wg, wu, wd, x):
    tm, tf = TM_DENSE, TF_DENSE
    row = lambda i, f: (i, 0)
    return pl.pallas_call(
        _ffn_kernel,
        grid=(N_TOK // tm, D_FF // tf),
        in_specs=[
            pl.BlockSpec((tm, D_MODEL), row),
            pl.BlockSpec((D_MODEL, tf), lambda i, f: (0, f)),
            pl.BlockSpec((D_MODEL, tf), lambda i, f: (0, f)),
            pl.BlockSpec((tf, D_MODEL), lambda i, f: (f, 0)),
            pl.BlockSpec((tm, D_MODEL), row),
        ],
        out_specs=pl.BlockSpec((tm, D_MODEL), row),
        out_shape=jax.ShapeDtypeStruct((N_TOK, D_MODEL), F32),
        scratch_shapes=[pltpu.VMEM((tm, D_MODEL), F32)],
        compiler_params=_cparams(("parallel", "arbitrary"), 56),
        name="ffn",
    )(h, wg, wu, wd, x)


def _pad_cols(w, width):
    return jnp.pad(w, ((0, 0), (0, width - w.shape[1])))


def _swap_halves(w):
    half = QK_ROPE // 2
    return jnp.concatenate([w[..., half:], w[..., :half]], axis=-1)


def _prep_w_in(w):
    kr = w[:, OFF_KR:OFF_GLU]
    return jnp.concatenate([
        w[:, OFF_POOL:OFF_KR],
        _pad_cols(kr, LANES), _pad_cols(_swap_halves(kr), LANES),
        w[:, OFF_GLU:]], axis=1).astype(BF16)


def _prep_w_uq(w_uq):
    def per_head(part):
        pad = LANES - part.shape[-1]
        return jnp.pad(part, ((0, 0), (0, 0), (0, pad))).reshape(Q_LORA, N_HEADS * LANES).astype(BF16)

    rope = w_uq[..., QK_NOPE:]
    return per_head(w_uq[..., :QK_NOPE]), per_head(rope), per_head(_swap_halves(rope))


def _prep_w_uk(w_uk):
    wt = jnp.transpose(w_uk, (1, 2, 0))
    return jnp.pad(wt, ((0, 0), (0, LANES - QK_NOPE), (0, 0))).astype(BF16)


def _prep_w_uv(w_uv):
    eye = jnp.eye(N_HEADS, dtype=w_uv.dtype)
    wbd = jnp.einsum('rhd,hg->hrgd', w_uv, eye)
    return wbd.reshape(LAT_COLS, ATTN_WIDTH).astype(BF16)


def _prep_w_pool(w_pool):
    n = len(POOL_WINDOWS)
    eye = jnp.eye(n, dtype=w_pool.dtype)
    return jnp.einsum('gcd,gk->gckd', w_pool, eye).reshape(POOL_WIDTH, POOL_WIDTH).astype(BF16)


def _rope_tables():
    inv = ROPE_THETA ** (-jnp.arange(0, QK_ROPE, 2, dtype=F32) / QK_ROPE)
    pos = jnp.concatenate([jnp.tile(jnp.arange(SEQ), BATCH),
                           jnp.tile(PAST_LEN + jnp.arange(DEC_SEQ), DEC_BATCH)])
    ang = pos.astype(F32)[:, None] * inv[None, :]
    cos, sin = jnp.cos(ang), jnp.sin(ang)
    cos_t = _pad_cols(jnp.concatenate([cos, cos], axis=1), LANES)
    sin_t = _pad_cols(jnp.concatenate([-sin, sin], axis=1), LANES)
    return cos_t, sin_t


def _split_hi_lo(w):
    hi = w.astype(BF16)
    lo = (w - hi.astype(F32)).astype(BF16)
    return hi, lo


def kernel(x_prompt, x_sample, cache_ckv, cache_krope, page_table, state_pool, state_conv, g_mix_norm, w_in, b_gate, w_pool, s_pool, g_q_lat, w_uq, g_kv_lat, w_uk, w_uv, w_dw, b_dw, g_conv_ln, b_conv_ln, w_br, w_out, g_ffn_norm, w_d_gate, w_d_up, w_d_down, w_router, w_e_gate, w_e_up, w_e_down, g_final):
    x = jnp.concatenate([x_prompt.reshape(N_PROMPT, D_MODEL),
                         x_sample.reshape(N_SAMPLE, D_MODEL)], axis=0)
    cos_t, sin_t = _rope_tables()
    pt_flat = page_table.reshape(-1)
    cache_krope_t = jnp.swapaxes(cache_krope, 2, 3)
    expert_w = (w_e_gate.astype(BF16), w_e_up.astype(BF16), w_e_down.astype(BF16))

    ckv_p, kr_p, pool_p, conv_p = [], [], [], []
    ckv_s, kr_s, pool_s, conv_s = [], [], [], []
    for l in range(DEPTH):
        u_pool, c_q, kvb, c_kv, k_r, z, gates = _inproj(
            x, g_mix_norm[l][None], _prep_w_in(w_in[l]), b_gate[l][None],
            g_q_lat[l][None], g_kv_lat[l][None], cos_t, sin_t)

        q_args = (c_q, *_prep_w_uq(w_uq[l]), _prep_w_uk(w_uk[l]), cos_t, sin_t)
        qt_p = _qproj(True, *q_args)
        q_s = _qproj(False, *q_args)

        wbd = _prep_w_pool(w_pool[l])
        sp = s_pool[l][None]
        u_p = u_pool[:N_PROMPT].reshape(BATCH, SEQ, POOL_WIDTH)
        u_s = u_pool[N_PROMPT:].reshape(DEC_BATCH, DEC_SEQ, POOL_WIDTH)
        pool_ext = jnp.concatenate(
            [jnp.zeros((DEC_BATCH, POOL_HALO - POOL_BUF, POOL_WIDTH), F32), state_pool[l], u_s], axis=1)
        o_pool = (_pool_prompt(u_p, wbd, sp).reshape(N_PROMPT, POOL_WIDTH),
                  _pool_sample(pool_ext, wbd, sp).reshape(N_SAMPLE, POOL_WIDTH))

        z_p = z[:N_PROMPT].reshape(BATCH, SEQ, CONV_WIDTH)
        z_s = z[N_PROMPT:].reshape(DEC_BATCH, DEC_SEQ, CONV_WIDTH)
        conv_ext = jnp.concatenate(
            [jnp.zeros((DEC_BATCH, CONV_HALO - CONV_BUF, CONV_WIDTH), F32), state_conv[l], z_s], axis=1)
        conv_args = (w_dw[l], b_dw[l][None], g_conv_ln[l][None], b_conv_ln[l][None])
        o_conv = (_conv_prompt(z_p, *conv_args).reshape(N_PROMPT, CONV_WIDTH),
                  _conv_sample(conv_ext, *conv_args).reshape(N_SAMPLE, CONV_WIDTH))

        nblk = SEQ // TQ
        kv_p = kvb[:N_PROMPT].reshape(BATCH, nblk, TQ, Q_HEAD)
        vt_p = jnp.concatenate([jnp.swapaxes(kv_p[..., :KV_LORA], 2, 3),
                                jnp.ones((BATCH, nblk, Q_HEAD - KV_LORA, TQ), BF16)], axis=2)
        lat_p = _attn_prompt(qt_p, kv_p, vt_p)
        q_s = q_s.reshape(DEC_BATCH, DEC_SEQ, N_HEADS, Q_HEAD)
        q_s = jnp.swapaxes(q_s, 1, 2).reshape(DEC_BATCH, N_HEADS * DEC_SEQ, Q_HEAD)
        kv_s = kvb[N_PROMPT:].reshape(DEC_BATCH, DEC_SEQ, Q_HEAD)
        kc_new = jnp.pad(kv_s[..., :KV_LORA], ((0, 0), (0, LANES - DEC_SEQ), (0, 0)))
        krt_new = jnp.pad(jnp.swapaxes(kv_s[..., KV_LORA:KV_LORA + QK_ROPE], 1, 2),
                          ((0, 0), (0, 0), (0, LANES - DEC_SEQ)))
        lat_s = _attn_sample(l, pt_flat, q_s, kc_new, krt_new, cache_ckv, cache_krope_t)
        lat_s = jnp.swapaxes(lat_s.reshape(DEC_BATCH, N_HEADS, DEC_SEQ, KV_LORA), 1, 2)
        lat = (lat_p.reshape(N_PROMPT, LAT_COLS), lat_s.reshape(N_SAMPLE, LAT_COLS))

        is_moe = l % 2 == 1
        wr = _pad_cols(w_router[l // 2], LANES) if is_moe else jnp.zeros((D_MODEL, LANES), F32)
        wrh, wrl = _split_hi_lo(wr)
        x, h, route = _merge(is_moe, o_pool, lat, o_conv, gates, x, _prep_w_uv(w_uv[l]),
                             w_br[l].astype(BF16), w_out[l].astype(BF16), g_ffn_norm[l][None],
                             wrh, wrl)
        if is_moe:
            pos, tile_expert, n_active = _route(route)
            xs = _dispatch(pos, h, jnp.zeros((M_SORTED,) + ROW_TILE, F32))
            ys = _expert_ffn(l // 2, tile_expert, n_active, xs, *expert_w)
            x = _combine(pos, ys, x, route, g_final[None], final_norm=(l == DEPTH - 1))
        else:
            x = _ffn(h, w_d_gate[l // 2].astype(BF16), w_d_up[l // 2].astype(BF16),
                     w_d_down[l // 2].astype(BF16), x)

        ckv_p.append(c_kv[:N_PROMPT].reshape(BATCH, SEQ, KV_LORA))
        kr_p.append(k_r[:N_PROMPT].reshape(BATCH, SEQ, QK_ROPE))
        pool_p.append(u_p[:, -POOL_BUF:])
        conv_p.append(z_p[:, -CONV_BUF:])
        ckv_s.append(c_kv[N_PROMPT:].reshape(DEC_BATCH, DEC_SEQ, KV_LORA))
        kr_s.append(k_r[N_PROMPT:].reshape(DEC_BATCH, DEC_SEQ, QK_ROPE))
        pool_s.append(jnp.concatenate([state_pool[l], u_s], axis=1)[:, -POOL_BUF:])
        conv_s.append(jnp.concatenate([state_conv[l], z_s], axis=1)[:, -CONV_BUF:])

    assert DEPTH % 2 == 0, "the last layer is an expert layer, whose combine applies the final norm"
    y = x
    return (y[:N_PROMPT].reshape(BATCH, SEQ, D_MODEL),
            y[N_PROMPT:].reshape(DEC_BATCH, DEC_SEQ, D_MODEL),
            jnp.stack(ckv_p), jnp.stack(kr_p), jnp.stack(pool_p), jnp.stack(conv_p),
            jnp.stack(ckv_s), jnp.stack(kr_s), jnp.stack(pool_s), jnp.stack(conv_s))
```

```python
import functools

import jax
import jax.numpy as jnp
from jax import lax
from jax.experimental import pallas as pl
from jax.experimental.pallas import tpu as pltpu

F32 = jnp.float32
BF16 = jnp.bfloat16

D_MODEL = 1024
BATCH = 8
SEQ = 2048
DEPTH = 4
DEC_BATCH = 128
DEC_SEQ = 8
PAST_LEN = 8192
PAGE_SIZE = 128
N_PAGES = PAST_LEN // PAGE_SIZE

POOL_WINDOWS = (2, 4, 8, 16)
POOL_GROUP = 64
POOL_WIDTH = 256
POOL_BUF = 15

N_HEADS = 8
QK_NOPE = 64
QK_ROPE = 32
V_DIM = 64
Q_LORA = 256
KV_LORA = 128
ROPE_THETA = 10000.0
ATTN_WIDTH = N_HEADS * V_DIM
ATTN_SCALE = (QK_NOPE + QK_ROPE) ** -0.5

CONV_WIDTH = 256
CONV_K = 31
CONV_BUF = CONV_K - 1

N_BRANCH = 3
OFF_POOL = 0
OFF_Q = OFF_POOL + POOL_WIDTH
OFF_KV = OFF_Q + Q_LORA
OFF_KR = OFF_KV + KV_LORA
OFF_GLU = OFF_KR + QK_ROPE
OFF_GATE = OFF_GLU + 2 * CONV_WIDTH

D_FF = 2816
N_EXPERTS = 8
D_FF_EXPERT = 3584

EPS = 1e-6
NEG_INF = -1e30

N_PROMPT = BATCH * SEQ
N_SAMPLE = DEC_BATCH * DEC_SEQ
N_TOK = N_PROMPT + N_SAMPLE

LANES = 128
VMEM_BYTES_V7X = 64 * 1024 * 1024

C_POOL = 0
C_Q = C_POOL + POOL_WIDTH
C_KV = C_Q + Q_LORA
C_KRA = C_KV + KV_LORA
C_KRB = C_KRA + LANES
C_GLU = C_KRB + LANES
C_GATE = C_GLU + 2 * CONV_WIDTH
C_END = C_GATE + N_BRANCH * D_MODEL

Q_HEAD = 2 * LANES
Q_COLS = N_HEADS * Q_HEAD
LAT_COLS = N_HEADS * KV_LORA

TM_TOKEN = 512
TM_DENSE = 512
TF_DENSE = 1408
TM_EXPERT = 1024
TF_EXPERT = 512
TOP_K = 2
N_PAIRS = TOP_K * N_TOK
EXPERT_TILES = N_PAIRS // TM_EXPERT + N_EXPERTS
M_SORTED = EXPERT_TILES * TM_EXPERT
ROW_TILE = (8, LANES)
DMA_CHUNK = 1024
PAGE_BUFFERS = 4
TQ = 256
ATTN_ROWS = N_HEADS * TQ
LOG2_E = 1.4426950408889634
Q_SCALE = ATTN_SCALE * LOG2_E
POOL_HALO = 16
CONV_HALO = 32
SEQ_CHUNK = 256
CONV_ROWS = 64
SAMPLE_BB = 16


def _cparams(semantics, vmem_mib):
    assert vmem_mib * 1024 * 1024 < VMEM_BYTES_V7X
    return pltpu.CompilerParams(dimension_semantics=semantics,
                                vmem_limit_bytes=vmem_mib * 1024 * 1024)


def _rms(x, g):
    return x * lax.rsqrt(jnp.mean(x * x, axis=-1, keepdims=True) + EPS) * g


def _dot(a, b):
    return jnp.dot(a, b, preferred_element_type=F32)


def _dot_nt(a, b):
    return lax.dot_general(a, b, (((1,), (1,)), ((), ())), preferred_element_type=F32)


def _inproj_kernel(x_ref, g_ref, w_ref, bg_ref, gq_ref, gkv_ref, cos_ref, sin_ref,
                   up_ref, cq_ref, kvb_ref, ckv_ref, kr_ref, z_ref, gate_ref):
    h = _rms(x_ref[...], g_ref[...]).astype(BF16)

    def proj(lo, hi):
        return _dot(h, w_ref[:, lo:hi])

    up_ref[...] = proj(C_POOL, C_Q)
    cq_ref[...] = _rms(proj(C_Q, C_KV), gq_ref[...]).astype(BF16)
    ckv = _rms(proj(C_KV, C_KRA), gkv_ref[...])
    kr = proj(C_KRA, C_KRB) * cos_ref[...] + proj(C_KRB, C_GLU) * sin_ref[...]
    ckv_ref[...] = ckv
    kr_ref[...] = kr[:, :QK_ROPE]
    kvb_ref[:, :KV_LORA] = ckv.astype(BF16)
    kvb_ref[:, KV_LORA:] = kr.astype(BF16)
    glu = proj(C_GLU, C_GATE)
    z_ref[...] = glu[:, :CONV_WIDTH] * jax.nn.sigmoid(glu[:, CONV_WIDTH:])
    for c in range(N_BRANCH):
        lo = c * D_MODEL
        g = proj(C_GATE + lo, C_GATE + lo + D_MODEL) + bg_ref[:, lo:lo + D_MODEL]
        gate_ref[:, lo:lo + D_MODEL] = jax.nn.sigmoid(g).astype(BF16)


def _inproj(x, g, w, bg, gq, gkv, cos, sin):
    tm = TM_TOKEN
    row = lambda i: (i, 0)
    fix = lambda i: (0, 0)
    return pl.pallas_call(
        _inproj_kernel,
        grid=(N_TOK // tm,),
        in_specs=[
            pl.BlockSpec((tm, D_MODEL), row),
            pl.BlockSpec((1, D_MODEL), fix),
            pl.BlockSpec((D_MODEL, C_END), fix),
            pl.BlockSpec((1, N_BRANCH * D_MODEL), fix),
            pl.BlockSpec((1, Q_LORA), fix),
            pl.BlockSpec((1, KV_LORA), fix),
            pl.BlockSpec((tm, LANES), row),
            pl.BlockSpec((tm, LANES), row),
        ],
        out_specs=[
            pl.BlockSpec((tm, POOL_WIDTH), row),
            pl.BlockSpec((tm, Q_LORA), row),
            pl.BlockSpec((tm, Q_HEAD), row),
            pl.BlockSpec((tm, KV_LORA), row),
            pl.BlockSpec((tm, QK_ROPE), row),
            pl.BlockSpec((tm, CONV_WIDTH), row),
            pl.BlockSpec((tm, N_BRANCH * D_MODEL), row),
        ],
        out_shape=[
            jax.ShapeDtypeStruct((N_TOK, POOL_WIDTH), F32),
            jax.ShapeDtypeStruct((N_TOK, Q_LORA), BF16),
            jax.ShapeDtypeStruct((N_TOK, Q_HEAD), BF16),
            jax.ShapeDtypeStruct((N_TOK, KV_LORA), F32),
            jax.ShapeDtypeStruct((N_TOK, QK_ROPE), F32),
            jax.ShapeDtypeStruct((N_TOK, CONV_WIDTH), F32),
            jax.ShapeDtypeStruct((N_TOK, N_BRANCH * D_MODEL), BF16),
        ],
        compiler_params=_cparams(("parallel",), 56),
        name="inproj",
    )(x, g, w, bg, gq, gkv, cos, sin)


def _q_heads(cq_ref, wn_ref, wa_ref, wb_ref, wuk_ref, cos_ref, sin_ref):
    cq = cq_ref[...]
    qn = _dot(cq, wn_ref[...]).astype(BF16)
    ra = _dot(cq, wa_ref[...])
    rb = _dot(cq, wb_ref[...])
    cos = cos_ref[...]
    sin = sin_ref[...]
    for h in range(N_HEADS):
        sl = slice(h * LANES, (h + 1) * LANES)
        yield (h, _dot(qn[:, sl], wuk_ref[h]) * Q_SCALE,
               (ra[:, sl] * cos + rb[:, sl] * sin) * Q_SCALE)


def _qproj_sample_kernel(*refs):
    q_ref = refs[-1]
    for h, q_abs, q_rope in _q_heads(*refs[:-1]):
        q_ref[:, h * Q_HEAD:h * Q_HEAD + LANES] = q_abs.astype(BF16)
        q_ref[:, h * Q_HEAD + LANES:(h + 1) * Q_HEAD] = q_rope.astype(BF16)


def _qproj_prompt_kernel(*refs):
    qt_ref = refs[-1]
    for h, q_abs, q_rope in _q_heads(*refs[:-1]):
        for c in range(TM_TOKEN // TQ):
            rows = slice(c * TQ, (c + 1) * TQ)
            cols = slice(h * TQ, (h + 1) * TQ)
            qt_ref[0, c, :LANES, cols] = q_abs[rows].T.astype(BF16)
            qt_ref[0, c, LANES:, cols] = q_rope[rows].T.astype(BF16)


def _qproj(for_prompt, cq, wn, wa, wb, wuk, cos, sin):
    tm = TM_TOKEN
    fix = lambda i: (0, 0)
    if for_prompt:
        n_steps, first = N_PROMPT // tm, 0
        tiles = tm // TQ
        per_seq = SEQ // tm
        out_spec = pl.BlockSpec((1, tiles, Q_HEAD, ATTN_ROWS),
                                lambda i: (i // per_seq, i % per_seq, 0, 0))
        out_shape = jax.ShapeDtypeStruct((BATCH, SEQ // TQ, Q_HEAD, ATTN_ROWS), BF16)
    else:
        n_steps, first = N_SAMPLE // tm, N_PROMPT // tm
        out_spec = pl.BlockSpec((tm, Q_COLS), lambda i: (i, 0))
        out_shape = jax.ShapeDtypeStruct((N_SAMPLE, Q_COLS), BF16)
    row = lambda i: (first + i, 0)
    return pl.pallas_call(
        _qproj_prompt_kernel if for_prompt else _qproj_sample_kernel,
        grid=(n_steps,),
        in_specs=[
            pl.BlockSpec((tm, Q_LORA), row),
            pl.BlockSpec((Q_LORA, N_HEADS * LANES), fix),
            pl.BlockSpec((Q_LORA, N_HEADS * LANES), fix),
            pl.BlockSpec((Q_LORA, N_HEADS * LANES), fix),
            pl.BlockSpec((N_HEADS, LANES, KV_LORA), lambda i: (0, 0, 0)),
            pl.BlockSpec((tm, LANES), row),
            pl.BlockSpec((tm, LANES), row),
        ],
        out_specs=out_spec,
        out_shape=out_shape,
        compiler_params=_cparams(("parallel",), 40),
        name="qproj_prompt" if for_prompt else "qproj_sample",
    )(cq, wn, wa, wb, wuk, cos, sin)


def _pool_compute(ext_ref, n_rows, cnt_of_window, wbd_ref, sp_ref):
    def ld(j):
        return ext_ref[:, pl.ds(POOL_HALO - j, n_rows), :]

    tok = ld(0)
    run = tok
    sums = {}
    for j in range(1, max(POOL_WINDOWS)):
        run = run + ld(j)
        if j + 1 in POOL_WINDOWS:
            sums[j + 1] = run
    lane = lax.broadcasted_iota(jnp.int32, tok.shape, 2)
    pooled = sums[POOL_WINDOWS[-1]] / cnt_of_window(POOL_WINDOWS[-1])
    for g in range(len(POOL_WINDOWS) - 2, -1, -1):
        w = POOL_WINDOWS[g]
        pooled = jnp.where(lane < (g + 1) * POOL_GROUP, sums[w] / cnt_of_window(w), pooled)
    pooled = (pooled - tok).reshape(-1, POOL_WIDTH).astype(BF16)
    return _dot(pooled, wbd_ref[...]) * sp_ref[...]


def _pool_prompt_kernel(halo_ref, u_ref, wbd_ref, sp_ref, o_ref, ext_ref):
    i = pl.program_id(1)
    ext_ref[:, :POOL_HALO, :] = jnp.where(i > 0, halo_ref[...], 0.0)
    ext_ref[:, POOL_HALO:, :] = u_ref[...]
    pos = lax.broadcasted_iota(jnp.int32, (1, SEQ_CHUNK, 1), 1) + i * SEQ_CHUNK

    def cnt(w):
        return jnp.minimum(pos + 1, w).astype(F32)

    o_ref[0] = _pool_compute(ext_ref, SEQ_CHUNK, cnt, wbd_ref, sp_ref).astype(BF16)


def _pool_prompt(u, wbd, sp):
    hb = SEQ_CHUNK // POOL_HALO
    return pl.pallas_call(
        _pool_prompt_kernel,
        grid=(BATCH, SEQ // SEQ_CHUNK),
        in_specs=[
            pl.BlockSpec((1, POOL_HALO, POOL_WIDTH),
                         lambda b, i: (b, jnp.maximum(i * hb - 1, 0), 0)),
            pl.BlockSpec((1, SEQ_CHUNK, POOL_WIDTH), lambda b, i: (b, i, 0)),
            pl.BlockSpec((POOL_WIDTH, POOL_WIDTH), lambda b, i: (0, 0)),
            pl.BlockSpec((1, POOL_WIDTH), lambda b, i: (0, 0)),
        ],
        out_specs=pl.BlockSpec((1, SEQ_CHUNK, POOL_WIDTH), lambda b, i: (b, i, 0)),
        out_shape=jax.ShapeDtypeStruct((BATCH, SEQ, POOL_WIDTH), BF16),
        scratch_shapes=[pltpu.VMEM((1, POOL_HALO + SEQ_CHUNK, POOL_WIDTH), F32)],
        compiler_params=_cparams(("parallel", "parallel"), 32),
        name="pool_prompt",
    )(u, u, wbd, sp)


def _pool_sample_kernel(ext_ref, wbd_ref, sp_ref, o_ref):
    out = _pool_compute(ext_ref, DEC_SEQ, lambda w: float(w), wbd_ref, sp_ref)
    o_ref[...] = out.reshape(SAMPLE_BB, DEC_SEQ, POOL_WIDTH).astype(BF16)


def _pool_sample(ext, wbd, sp):
    return pl.pallas_call(
        _pool_sample_kernel,
        grid=(DEC_BATCH // SAMPLE_BB,),
        in_specs=[
            pl.BlockSpec((SAMPLE_BB, POOL_HALO + DEC_SEQ, POOL_WIDTH), lambda i: (i, 0, 0)),
            pl.BlockSpec((POOL_WIDTH, POOL_WIDTH), lambda i: (0, 0)),
            pl.BlockSpec((1, POOL_WIDTH), lambda i: (0, 0)),
        ],
        out_specs=pl.BlockSpec((SAMPLE_BB, DEC_SEQ, POOL_WIDTH), lambda i: (i, 0, 0)),
        out_shape=jax.ShapeDtypeStruct((DEC_BATCH, DEC_SEQ, POOL_WIDTH), BF16),
        compiler_params=_cparams(("parallel",), 32),
        name="pool_sample",
    )(ext, wbd, sp)


def _conv_compute(ext_ref, row0, n_rows, wdw_ref, bdw_ref, g_ref, b_ref):
    lead = CONV_HALO - CONV_BUF
    acc = None
    for k in range(CONV_K):
        term = ext_ref[:, pl.ds(row0 + lead + k, n_rows), :] * wdw_ref[k:k + 1, :]
        acc = term if acc is None else acc + term
    zc = acc + bdw_ref[...]
    mu = jnp.mean(zc, axis=-1, keepdims=True)
    xc = zc - mu
    y = xc * lax.rsqrt(jnp.mean(xc * xc, axis=-1, keepdims=True) + EPS)
    y = y * g_ref[...] + b_ref[...]
    return y * jax.nn.sigmoid(y)


def _conv_prompt_kernel(halo_ref, z_ref, wdw_ref, bdw_ref, g_ref, b_ref, o_ref, ext_ref):
    i = pl.program_id(1)
    ext_ref[:, :CONV_HALO, :] = jnp.where(i > 0, halo_ref[...], 0.0)
    ext_ref[:, CONV_HALO:, :] = z_ref[...]
    for r0 in range(0, SEQ_CHUNK, CONV_ROWS):
        o_ref[:, r0:r0 + CONV_ROWS, :] = _conv_compute(
            ext_ref, r0, CONV_ROWS, wdw_ref, bdw_ref, g_ref, b_ref).astype(BF16)


def _conv_prompt(z, wdw, bdw, g, b):
    hb = SEQ_CHUNK // CONV_HALO
    vec = pl.BlockSpec((1, CONV_WIDTH), lambda bb, i: (0, 0))
    return pl.pallas_call(
        _conv_prompt_kernel,
        grid=(BATCH, SEQ // SEQ_CHUNK),
        in_specs=[
            pl.BlockSpec((1, CONV_HALO, CONV_WIDTH),
                         lambda bb, i: (bb, jnp.maximum(i * hb - 1, 0), 0)),
            pl.BlockSpec((1, SEQ_CHUNK, CONV_WIDTH), lambda bb, i: (bb, i, 0)),
            pl.BlockSpec((CONV_K, CONV_WIDTH), lambda bb, i: (0, 0)),
            vec, vec, vec,
        ],
        out_specs=pl.BlockSpec((1, SEQ_CHUNK, CONV_WIDTH), lambda bb, i: (bb, i, 0)),
        out_shape=jax.ShapeDtypeStruct((BATCH, SEQ, CONV_WIDTH), BF16),
        scratch_shapes=[pltpu.VMEM((1, CONV_HALO + SEQ_CHUNK, CONV_WIDTH), F32)],
        compiler_params=_cparams(("parallel", "parallel"), 32),
        name="conv_prompt",
    )(z, z, wdw, bdw, g, b)


def _conv_sample_kernel(ext_ref, wdw_ref, bdw_ref, g_ref, b_ref, o_ref):
    o_ref[...] = _conv_compute(ext_ref, 0, DEC_SEQ, wdw_ref, bdw_ref, g_ref, b_ref).astype(BF16)


def _conv_sample(ext, wdw, bdw, g, b):
    vec = pl.BlockSpec((1, CONV_WIDTH), lambda i: (0, 0))
    return pl.pallas_call(
        _conv_sample_kernel,
        grid=(DEC_BATCH // SAMPLE_BB,),
        in_specs=[
            pl.BlockSpec((SAMPLE_BB, CONV_HALO + DEC_SEQ, CONV_WIDTH), lambda i: (i, 0, 0)),
            pl.BlockSpec((CONV_K, CONV_WIDTH), lambda i: (0, 0)),
            vec, vec, vec,
        ],
        out_specs=pl.BlockSpec((SAMPLE_BB, DEC_SEQ, CONV_WIDTH), lambda i: (i, 0, 0)),
        out_shape=jax.ShapeDtypeStruct((DEC_BATCH, DEC_SEQ, CONV_WIDTH), BF16),
        compiler_params=_cparams(("parallel",), 32),
        name="conv_sample",
    )(ext, wdw, bdw, g, b)


def _attn_prompt_kernel(qt_ref, k_ref, vt_ref, o_ref, m_ref, acc_ref):
    qi = pl.program_id(1)
    qt = qt_ref[0, 0]
    m_ref[...] = jnp.full(m_ref.shape, NEG_INF, F32)
    acc_ref[...] = jnp.zeros(acc_ref.shape, F32)

    def block(j, on_diagonal):
        s = _dot(k_ref[0, j], qt)
        if on_diagonal:
            tok = lax.broadcasted_iota(jnp.int32, s.shape, 1) & (TQ - 1)
            key = lax.broadcasted_iota(jnp.int32, s.shape, 0)
            s = jnp.where(key <= tok, s, NEG_INF)
        m_old = m_ref[...]
        m_new = jnp.maximum(m_old, jnp.max(s, axis=0, keepdims=True))
        p = jnp.exp2(s - m_new).astype(BF16)
        acc_ref[...] = jnp.exp2(m_old - m_new) * acc_ref[...] + _dot(vt_ref[0, j], p)
        m_ref[...] = m_new

    def below_diagonal(j, carry):
        block(j, False)
        return carry

    lax.fori_loop(0, qi, below_diagonal, 0)
    block(qi, True)
    acc = acc_ref[...]
    out = (acc[:KV_LORA] / acc[KV_LORA:]).T.astype(BF16)
    for h in range(N_HEADS):
        o_ref[0, :, h * KV_LORA:(h + 1) * KV_LORA] = out[h * TQ:(h + 1) * TQ]


def _attn_prompt(qt, k, vt):
    nblk = SEQ // TQ
    return pl.pallas_call(
        _attn_prompt_kernel,
        grid=(BATCH, nblk),
        in_specs=[
            pl.BlockSpec((1, 1, Q_HEAD, ATTN_ROWS), lambda b, i: (b, i, 0, 0)),
            pl.BlockSpec((1, nblk, TQ, Q_HEAD), lambda b, i: (b, 0, 0, 0)),
            pl.BlockSpec((1, nblk, Q_HEAD, TQ), lambda b, i: (b, 0, 0, 0)),
        ],
        out_specs=pl.BlockSpec((1, TQ, LAT_COLS), lambda b, i: (b, i, 0)),
        out_shape=jax.ShapeDtypeStruct((BATCH, SEQ, LAT_COLS), BF16),
        scratch_shapes=[pltpu.VMEM((1, ATTN_ROWS), F32), pltpu.VMEM((Q_HEAD, ATTN_ROWS), F32)],
        compiler_params=_cparams(("parallel", "arbitrary"), 40),
        name="attn_prompt",
    )(qt, k, vt)


def _attn_sample_kernel(layer, pt_ref, q_ref, kcn_ref, krn_ref, ckv_hbm, krt_hbm, o_ref, *scratch):
    b = pl.program_id(0)
    n = pl.num_programs(0)
    ckbufs = scratch[:PAGE_BUFFERS]
    krbufs = scratch[PAGE_BUFFERS:2 * PAGE_BUFFERS]
    semc, semr = scratch[2 * PAGE_BUFFERS:]
    ahead = PAGE_BUFFERS - 1

    def issue(seq, slot):
        for p in range(N_PAGES):
            pid = pt_ref[seq * N_PAGES + p]
            pltpu.make_async_copy(ckv_hbm.at[layer, pid], ckbufs[slot].at[p],
                                  semc.at[slot]).start(priority=p % 2)
            pltpu.make_async_copy(krt_hbm.at[layer, pid], krbufs[slot].at[p],
                                  semr.at[slot]).start(priority=p % 2)

    def drain(slot):
        pltpu.make_async_copy(ckv_hbm.at[layer, pl.ds(0, N_PAGES)], ckbufs[slot], semc.at[slot]).wait()
        pltpu.make_async_copy(krt_hbm.at[layer, pl.ds(0, N_PAGES)], krbufs[slot], semr.at[slot]).wait()

    @pl.when(b == 0)
    def _():
        for first in range(ahead):
            issue(first, first)

    def update(state, s, kc):
        m, l, acc = state
        m_new = jnp.maximum(m, jnp.max(s, axis=-1, keepdims=True))
        alpha = jnp.exp2(m - m_new)
        p = jnp.exp2(s - m_new)
        return (m_new, alpha * l + jnp.sum(p, axis=-1, keepdims=True),
                alpha * acc + _dot(p.astype(BF16), kc))

    def step(slot):
        drain(slot)
        issue(jnp.minimum(b + ahead, n - 1), (slot + ahead) % PAGE_BUFFERS)
        q = q_ref[0]
        qa = q[:, :KV_LORA]
        qr = q[:, KV_LORA:KV_LORA + QK_ROPE]
        rows = N_HEADS * DEC_SEQ
        state = (jnp.full((rows, 1), NEG_INF, F32), jnp.zeros((rows, 1), F32),
                 jnp.zeros((rows, KV_LORA), F32))
        kc = ckbufs[slot][...].reshape(PAST_LEN, KV_LORA).astype(BF16)
        kr = jnp.concatenate([krbufs[slot][p] for p in range(N_PAGES)], axis=-1).astype(BF16)
        state = update(state, _dot_nt(qa, kc) + _dot(qr, kr), kc)
        kcn = kcn_ref[0]
        s = _dot_nt(qa, kcn) + _dot(qr, krn_ref[0])
        t = lax.broadcasted_iota(jnp.int32, s.shape, 0) & (DEC_SEQ - 1)
        key = lax.broadcasted_iota(jnp.int32, s.shape, 1)
        _, l, acc = update(state, jnp.where(key <= t, s, NEG_INF), kcn)
        o_ref[0] = (acc / l).astype(BF16)

        @pl.when(b == n - 1)
        def _():
            for later in range(1, PAGE_BUFFERS):
                drain((slot + later) % PAGE_BUFFERS)

    for slot in range(PAGE_BUFFERS):
        pl.when(b % PAGE_BUFFERS == slot)(functools.partial(step, slot))


def _attn_sample(layer, page_table, q, kc_new, krt_new, cache_ckv, cache_krope_t):
    page_ck = pltpu.VMEM((N_PAGES, PAGE_SIZE, KV_LORA), F32)
    page_kr = pltpu.VMEM((N_PAGES, QK_ROPE, PAGE_SIZE), F32)
    grid_spec = pltpu.PrefetchScalarGridSpec(
        num_scalar_prefetch=1,
        grid=(DEC_BATCH,),
        in_specs=[pl.BlockSpec((1, N_HEADS * DEC_SEQ, Q_HEAD), lambda b, pt: (b, 0, 0)),
                  pl.BlockSpec((1, LANES, KV_LORA), lambda b, pt: (b, 0, 0)),
                  pl.BlockSpec((1, QK_ROPE, LANES), lambda b, pt: (b, 0, 0)),
                  pl.BlockSpec(memory_space=pl.ANY),
                  pl.BlockSpec(memory_space=pl.ANY)],
        out_specs=pl.BlockSpec((1, N_HEADS * DEC_SEQ, KV_LORA), lambda b, pt: (b, 0, 0)),
        scratch_shapes=([page_ck] * PAGE_BUFFERS + [page_kr] * PAGE_BUFFERS
                        + [pltpu.SemaphoreType.DMA((PAGE_BUFFERS,))] * 2),
    )
    return pl.pallas_call(
        functools.partial(_attn_sample_kernel, layer),
        grid_spec=grid_spec,
        out_shape=jax.ShapeDtypeStruct((DEC_BATCH, N_HEADS * DEC_SEQ, KV_LORA), BF16),
        compiler_params=_cparams(("arbitrary",), 48),
        name="attn_sample",
    )(page_table, q, kc_new, krt_new, cache_ckv, cache_krope_t)


def _merge_kernel(with_router, op_p, op_s, lat_p, lat_s, oc_p, oc_s, gate_ref, x_ref, wuv_ref,
                  wbr_ref, wout_ref, gf_ref, wrh_ref, wrl_ref, xo_ref, h_ref, route_ref):
    in_prompt = pl.program_id(0) < N_PROMPT // TM_TOKEN

    def pick(prompt_ref, sample_ref):
        return jnp.where(in_prompt, prompt_ref[...], sample_ref[...])

    o_attn = _dot(pick(lat_p, lat_s), wuv_ref[...]).astype(BF16)
    br_a = _dot(pick(op_p, op_s), wbr_ref[:POOL_WIDTH, :])
    br_b = _dot(o_attn, wbr_ref[POOL_WIDTH:POOL_WIDTH + ATTN_WIDTH, :])
    br_c = _dot(pick(oc_p, oc_s), wbr_ref[POOL_WIDTH + ATTN_WIDTH:, :])
    merged = (gate_ref[:, :D_MODEL].astype(F32) * br_a
              + gate_ref[:, D_MODEL:2 * D_MODEL].astype(F32) * br_b
              + gate_ref[:, 2 * D_MODEL:].astype(F32) * br_c)
    xn = x_ref[...] + _dot(merged.astype(BF16), wout_ref[...])
    xo_ref[...] = xn
    hn = _rms(xn, gf_ref[...])
    if not with_router:
        h_ref[...] = hn.astype(BF16)
        route_ref[...] = jnp.zeros(route_ref.shape, F32)
        return
    for c in range(ROW_TILE[0]):
        h_ref[:, c, :] = hn[:, c * LANES:(c + 1) * LANES]
    hh = hn.astype(BF16)
    hl = (hn - hh.astype(F32)).astype(BF16)
    lg = _dot(hh, wrh_ref[...]) + _dot(hl, wrh_ref[...]) + _dot(hh, wrl_ref[...])
    lane = lax.broadcasted_iota(jnp.int32, lg.shape, 1).astype(F32)
    lg = jnp.where(lane < N_EXPERTS, lg, NEG_INF)
    m1 = jnp.max(lg, axis=-1, keepdims=True)
    i1 = jnp.min(jnp.where(lg == m1, lane, float(LANES)), axis=-1, keepdims=True)
    lg2 = jnp.where(lane == i1, NEG_INF, lg)
    m2 = jnp.max(lg2, axis=-1, keepdims=True)
    i2 = jnp.min(jnp.where(lg2 == m2, lane, float(LANES)), axis=-1, keepdims=True)
    e = jnp.exp(m2 - m1)
    w1 = 1.0 / (1.0 + e)
    w2 = e / (1.0 + e)
    route_ref[...] = jnp.where(lane == 0, i1, jnp.where(lane == 1, i2, jnp.where(
        lane == 2, w1, jnp.where(lane == 3, w2, 0.0))))


def _merge(with_router, o_pool, lat, o_conv, gates, x, wuv, wbr, wout, gf, wrh, wrl):
    tm = TM_TOKEN
    row = lambda i: (i, 0)
    fix = lambda i: (0, 0)
    prompt_tiles = N_PROMPT // tm
    prompt_row = lambda i: (jnp.minimum(i, prompt_tiles - 1), 0)
    sample_row = lambda i: (jnp.maximum(i - prompt_tiles, 0), 0)

    def pair(width):
        return [pl.BlockSpec((tm, width), prompt_row), pl.BlockSpec((tm, width), sample_row)]
    if with_router:
        h_spec = pl.BlockSpec((tm,) + ROW_TILE, lambda i: (i, 0, 0))
        h_shape = jax.ShapeDtypeStruct((N_TOK,) + ROW_TILE, F32)
    else:
        h_spec = pl.BlockSpec((tm, D_MODEL), row)
        h_shape = jax.ShapeDtypeStruct((N_TOK, D_MODEL), BF16)
    return pl.pallas_call(
        functools.partial(_merge_kernel, with_router),
        grid=(N_TOK // tm,),
        in_specs=pair(POOL_WIDTH) + pair(LAT_COLS) + pair(CONV_WIDTH) + [
            pl.BlockSpec((tm, N_BRANCH * D_MODEL), row),
            pl.BlockSpec((tm, D_MODEL), row),
            pl.BlockSpec((LAT_COLS, ATTN_WIDTH), fix),
            pl.BlockSpec((D_MODEL, D_MODEL), fix),
            pl.BlockSpec((D_MODEL, D_MODEL), fix),
            pl.BlockSpec((1, D_MODEL), fix),
            pl.BlockSpec((D_MODEL, LANES), fix),
            pl.BlockSpec((D_MODEL, LANES), fix),
        ],
        out_specs=[
            pl.BlockSpec((tm, D_MODEL), row),
            h_spec,
            pl.BlockSpec((tm, LANES), row),
        ],
        out_shape=[
            jax.ShapeDtypeStruct((N_TOK, D_MODEL), F32),
            h_shape,
            jax.ShapeDtypeStruct((N_TOK, LANES), F32),
        ],
        compiler_params=_cparams(("parallel",), 48),
        name="merge_router" if with_router else "merge",
    )(*o_pool, *lat, *o_conv, gates, x, wuv, wbr, wout, gf, wrh, wrl)


def _row_chunk_copy(src_ref, dst_ref, sem):
    return pltpu.make_async_copy(src_ref.at[pl.ds(0, DMA_CHUNK)], dst_ref.at[pl.ds(0, DMA_CHUNK)], sem)


def _dispatch_kernel(pos_ref, h_ref, init_ref, xs_ref, sem):
    del init_ref
    pair0 = pl.program_id(0) * DMA_CHUNK

    def issue(r, carry):
        pltpu.make_async_copy(h_ref.at[r], xs_ref.at[pos_ref[pair0 + r]], sem).start()
        return carry

    lax.fori_loop(0, DMA_CHUNK, issue, 0, unroll=8)
    _row_chunk_copy(h_ref, xs_ref, sem).wait()


def _dispatch(pos, h_rows, init):
    token_chunks = N_TOK // DMA_CHUNK
    grid_spec = pltpu.PrefetchScalarGridSpec(
        num_scalar_prefetch=1,
        grid=(N_PAIRS // DMA_CHUNK,),
        in_specs=[pl.BlockSpec((DMA_CHUNK,) + ROW_TILE,
                               lambda i, pos: (i % token_chunks, 0, 0)),
                  pl.BlockSpec(memory_space=pl.ANY)],
        out_specs=pl.BlockSpec(memory_space=pl.ANY),
        scratch_shapes=[pltpu.SemaphoreType.DMA(())],
    )
    return pl.pallas_call(
        _dispatch_kernel,
        grid_spec=grid_spec,
        out_shape=jax.ShapeDtypeStruct((M_SORTED,) + ROW_TILE, F32),
        input_output_aliases={2: 0},
        compiler_params=_cparams(("arbitrary",), 24),
        name="dispatch",
    )(pos, h_rows, init)


def _expert_ffn_kernel(te_ref, na_ref, xs_ref, wg_ref, wu_ref, wd_ref, ys_ref, xb_ref, acc_ref):
    del te_ref
    i = pl.program_id(0)
    f = pl.program_id(1)
    active = i < na_ref[0]

    @pl.when(active & (f == 0))
    def _():
        xb_ref[...] = jnp.concatenate(
            [xs_ref[:, c, :] for c in range(ROW_TILE[0])], axis=-1).astype(BF16)

    @pl.when(active)
    def _():
        x = xb_ref[...]
        a = _dot(x, wg_ref[...])
        b = _dot(x, wu_ref[...])
        y = _dot((a * jax.nn.sigmoid(a) * b).astype(BF16), wd_ref[...])

        @pl.when(f == 0)
        def _():
            acc_ref[...] = y

        @pl.when(f > 0)
        def _():
            acc_ref[...] += y

    @pl.when(f == pl.num_programs(1) - 1)
    def _():
        y = jnp.where(active, acc_ref[...], 0.0)
        for c in range(ROW_TILE[0]):
            ys_ref[:, c, :] = y[:, c * LANES:(c + 1) * LANES]


def _expert_ffn(moe_layer, tile_expert, n_active, xs, wg, wu, wd):
    tm, tf = TM_EXPERT, TF_EXPERT
    nf = D_FF_EXPERT // tf

    def tile(i, na):
        return jnp.minimum(i, na[0] - 1)

    def hidden(i, f, na):
        return jnp.where(i < na[0], f, nf - 1)

    grid_spec = pltpu.PrefetchScalarGridSpec(
        num_scalar_prefetch=2,
        grid=(EXPERT_TILES, nf),
        in_specs=[
            pl.BlockSpec((tm,) + ROW_TILE, lambda i, f, te, na: (tile(i, na), 0, 0)),
            pl.BlockSpec((None, None, D_MODEL, tf),
                         lambda i, f, te, na: (moe_layer, te[tile(i, na)], 0, hidden(i, f, na))),
            pl.BlockSpec((None, None, D_MODEL, tf),
                         lambda i, f, te, na: (moe_layer, te[tile(i, na)], 0, hidden(i, f, na))),
            pl.BlockSpec((None, None, tf, D_MODEL),
                         lambda i, f, te, na: (moe_layer, te[tile(i, na)], hidden(i, f, na), 0)),
        ],
        out_specs=pl.BlockSpec((tm,) + ROW_TILE, lambda i, f, te, na: (i, 0, 0)),
        scratch_shapes=[pltpu.VMEM((tm, D_MODEL), BF16), pltpu.VMEM((tm, D_MODEL), F32)],
    )
    return pl.pallas_call(
        _expert_ffn_kernel,
        grid_spec=grid_spec,
        out_shape=jax.ShapeDtypeStruct((M_SORTED,) + ROW_TILE, F32),
        compiler_params=_cparams(("arbitrary", "arbitrary"), 56),
        name="expert_ffn",
    )(tile_expert, n_active, xs, wg, wu, wd)


def _combine_kernel(final_norm, pos_ref, ys_ref, x_ref, route_ref, g_ref, o_ref, buf_ref, sem):
    i = pl.program_id(0)
    n = pl.num_programs(0)
    tmc = DMA_CHUNK // TOP_K

    def issue(tile, slot):
        def body(r, carry):
            for k in range(TOP_K):
                pltpu.make_async_copy(ys_ref.at[pos_ref[k * N_TOK + tile * tmc + r]],
                                      buf_ref.at[slot, k * tmc + r], sem.at[slot]).start()
            return carry

        lax.fori_loop(0, tmc, body, 0, unroll=8)

    @pl.when(i == 0)
    def _():
        issue(0, 0)

    @pl.when(i + 1 < n)
    def _():
        issue(i + 1, (i + 1) % 2)

    slot = i % 2
    _row_chunk_copy(ys_ref, buf_ref.at[slot], sem.at[slot]).wait()
    w1 = route_ref[:, 2:3]
    w2 = route_ref[:, 3:4]
    for c in range(ROW_TILE[0]):
        cols = slice(c * LANES, (c + 1) * LANES)
        o_ref[:, cols] = x_ref[:, cols] + (w1 * buf_ref[slot, :tmc, c, :]
                                            + w2 * buf_ref[slot, tmc:, c, :])
    if final_norm:
        o_ref[...] = _rms(o_ref[...], g_ref[...])


def _combine(pos, ys, x, route, gain, final_norm):
    tmc = DMA_CHUNK // TOP_K
    grid_spec = pltpu.PrefetchScalarGridSpec(
        num_scalar_prefetch=1,
        grid=(N_TOK // tmc,),
        in_specs=[
            pl.BlockSpec(memory_space=pl.ANY),
            pl.BlockSpec((tmc, D_MODEL), lambda i, pos: (i, 0)),
            pl.BlockSpec((tmc, LANES), lambda i, pos: (i, 0)),
            pl.BlockSpec((1, D_MODEL), lambda i, pos: (0, 0)),
        ],
        out_specs=pl.BlockSpec((tmc, D_MODEL), lambda i, pos: (i, 0)),
        scratch_shapes=[pltpu.VMEM((2, DMA_CHUNK) + ROW_TILE, F32),
                        pltpu.SemaphoreType.DMA((2,))],
    )
    return pl.pallas_call(
        functools.partial(_combine_kernel, final_norm),
        grid_spec=grid_spec,
        out_shape=jax.ShapeDtypeStruct((N_TOK, D_MODEL), F32),
        compiler_params=_cparams(("arbitrary",), 40),
        name="combine_norm" if final_norm else "combine",
    )(pos, ys, x, route, gain)


def _route(route):
    experts = jnp.concatenate([route[:, 0], route[:, 1]]).astype(jnp.int32)
    one_hot = (experts[:, None] == jnp.arange(N_EXPERTS)[None, :]).astype(jnp.int32)
    running = jnp.cumsum(one_hot, axis=0)
    rank = jnp.sum(one_hot * running, axis=1) - 1
    tiles = (running[-1] + TM_EXPERT - 1) // TM_EXPERT
    tile_end = jnp.cumsum(tiles)
    pos = jnp.sum(one_hot * ((tile_end - tiles) * TM_EXPERT)[None, :], axis=1) + rank
    tile_ids = jnp.arange(EXPERT_TILES, dtype=jnp.int32)
    tile_expert = jnp.minimum(jnp.sum(tile_ids[:, None] >= tile_end[None, :], axis=1),
                              N_EXPERTS - 1)
    return pos.astype(jnp.int32), tile_expert.astype(jnp.int32), tile_end[-1:].astype(jnp.int32)


def _ffn_kernel(h_ref, wg_ref, wu_ref, wd_ref, x_ref, o_ref, acc_ref):
    f = pl.program_id(1)
    h = h_ref[...]
    a = _dot(h, wg_ref[...])
    b = _dot(h, wu_ref[...])
    y = _dot((a * jax.nn.sigmoid(a) * b).astype(BF16), wd_ref[...])

    @pl.when(f == 0)
    def _():
        acc_ref[...] = y

    @pl.when(f > 0)
    def _():
        acc_ref[...] += y

    @pl.when(f == pl.num_programs(1) - 1)
    def _():
        o_ref[...] = x_ref[...] + acc_ref[...]


def _ffn(h, wg, wu, wd, x):
    tm, tf = TM_DENSE, TF_DENSE
    row = lambda i, f: (i, 0)
    return pl.pallas_call(
        _ffn_kernel,
        grid=(N_TOK // tm, D_FF // tf),
        in_specs=[
            pl.BlockSpec((tm, D_MODEL), row),
            pl.BlockSpec((D_MODEL, tf), lambda i, f: (0, f)),
            pl.BlockSpec((D_MODEL, tf), lambda i, f: (0, f)),
            pl.BlockSpec((tf, D_MODEL), lambda i, f: (f, 0)),
            pl.BlockSpec((tm, D_MODEL), row),
        ],
        out_specs=pl.BlockSpec((tm, D_MODEL), row),
        out_shape=jax.ShapeDtypeStruct((N_TOK, D_MODEL), F32),
        scratch_shapes=[pltpu.VMEM((tm, D_MODEL), F32)],
        compiler_params=_cparams(("parallel", "arbitrary"), 56),
        name="ffn",
    )(h, wg, wu, wd, x)


def _pad_cols(w, width):
    return jnp.pad(w, ((0, 0), (0, width - w.shape[1])))


def _swap_halves(w):
    half = QK_ROPE // 2
    return jnp.concatenate([w[..., half:], w[..., :half]], axis=-1)


def _prep_w_in(w):
    kr = w[:, OFF_KR:OFF_GLU]
    return jnp.concatenate([
        w[:, OFF_POOL:OFF_KR],
        _pad_cols(kr, LANES), _pad_cols(_swap_halves(kr), LANES),
        w[:, OFF_GLU:]], axis=1).astype(BF16)


def _prep_w_uq(w_uq):
    def per_head(part):
        pad = LANES - part.shape[-1]
        return jnp.pad(part, ((0, 0), (0, 0), (0, pad))).reshape(Q_LORA, N_HEADS * LANES).astype(BF16)

    rope = w_uq[..., QK_NOPE:]
    return per_head(w_uq[..., :QK_NOPE]), per_head(rope), per_head(_swap_halves(rope))


def _prep_w_uk(w_uk):
    wt = jnp.transpose(w_uk, (1, 2, 0))
    return jnp.pad(wt, ((0, 0), (0, LANES - QK_NOPE), (0, 0))).astype(BF16)


def _prep_w_uv(w_uv):
    eye = jnp.eye(N_HEADS, dtype=w_uv.dtype)
    wbd = jnp.einsum('rhd,hg->hrgd', w_uv, eye)
    return wbd.reshape(LAT_COLS, ATTN_WIDTH).astype(BF16)


def _prep_w_pool(w_pool):
    n = len(POOL_WINDOWS)
    eye = jnp.eye(n, dtype=w_pool.dtype)
    return jnp.einsum('gcd,gk->gckd', w_pool, eye).reshape(POOL_WIDTH, POOL_WIDTH).astype(BF16)


def _rope_tables():
    inv = ROPE_THETA ** (-jnp.arange(0, QK_ROPE, 2, dtype=F32) / QK_ROPE)
    pos = jnp.concatenate([jnp.tile(jnp.arange(SEQ), BATCH),
                           jnp.tile(PAST_LEN + jnp.arange(DEC_SEQ), DEC_BATCH)])
    ang = pos.astype(F32)[:, None] * inv[None, :]
    cos, sin = jnp.cos(ang), jnp.sin(ang)
    cos_t = _pad_cols(jnp.concatenate([cos, cos], axis=1), LANES)
    sin_t = _pad_cols(jnp.concatenate([-sin, sin], axis=1), LANES)
    return cos_t, sin_t


def _split_hi_lo(w):
    hi = w.astype(BF16)
    lo = (w - hi.astype(F32)).astype(BF16)
    return hi, lo


def kernel(x_prompt, x_sample, cache_ckv, cache_krope, page_table, state_pool, state_conv, g_mix_norm, w_in, b_gate, w_pool, s_pool, g_q_lat, w_uq, g_kv_lat, w_uk, w_uv, w_dw, b_dw, g_conv_ln, b_conv_ln, w_br, w_out, g_ffn_norm, w_d_gate, w_d_up, w_d_down, w_router, w_e_gate, w_e_up, w_e_down, g_final):
    x = jnp.concatenate([x_prompt.reshape(N_PROMPT, D_MODEL),
                         x_sample.reshape(N_SAMPLE, D_MODEL)], axis=0)
    cos_t, sin_t = _rope_tables()
    pt_flat = page_table.reshape(-1)
    cache_krope_t = jnp.swapaxes(cache_krope, 2, 3)
    expert_w = (w_e_gate.astype(BF16), w_e_up.astype(BF16), w_e_down.astype(BF16))

    ckv_p, kr_p, pool_p, conv_p = [], [], [], []
    ckv_s, kr_s, pool_s, conv_s = [], [], [], []
    for l in range(DEPTH):
        u_pool, c_q, kvb, c_kv, k_r, z, gates = _inproj(
            x, g_mix_norm[l][None], _prep_w_in(w_in[l]), b_gate[l][None],
            g_q_lat[l][None], g_kv_lat[l][None], cos_t, sin_t)

        q_args = (c_q, *_prep_w_uq(w_uq[l]), _prep_w_uk(w_uk[l]), cos_t, sin_t)
        qt_p = _qproj(True, *q_args)
        q_s = _qproj(False, *q_args)

        wbd = _prep_w_pool(w_pool[l])
        sp = s_pool[l][None]
        u_p = u_pool[:N_PROMPT].reshape(BATCH, SEQ, POOL_WIDTH)
        u_s = u_pool[N_PROMPT:].reshape(DEC_BATCH, DEC_SEQ, POOL_WIDTH)
        pool_ext = jnp.concatenate(
            [jnp.zeros((DEC_BATCH, POOL_HALO - POOL_BUF, POOL_WIDTH), F32), state_pool[l], u_s], axis=1)
        o_pool = (_pool_prompt(u_p, wbd, sp).reshape(N_PROMPT, POOL_WIDTH),
                  _pool_sample(pool_ext, wbd, sp).reshape(N_SAMPLE, POOL_WIDTH))

        z_p = z[:N_PROMPT].reshape(BATCH, SEQ, CONV_WIDTH)
        z_s = z[N_PROMPT:].reshape(DEC_BATCH, DEC_SEQ, CONV_WIDTH)
        conv_ext = jnp.concatenate(
            [jnp.zeros((DEC_BATCH, CONV_HALO - CONV_BUF, CONV_WIDTH), F32), state_conv[l], z_s], axis=1)
        conv_args = (w_dw[l], b_dw[l][None], g_conv_ln[l][None], b_conv_ln[l][None])
        o_conv = (_conv_prompt(z_p, *conv_args).reshape(N_PROMPT, CONV_WIDTH),
                  _conv_sample(conv_ext, *conv_args).reshape(N_SAMPLE, CONV_WIDTH))

        nblk = SEQ // TQ
        kv_p = kvb[:N_PROMPT].reshape(BATCH, nblk, TQ, Q_HEAD)
        vt_p = jnp.concatenate([jnp.swapaxes(kv_p[..., :KV_LORA], 2, 3),
                                jnp.ones((BATCH, nblk, Q_HEAD - KV_LORA, TQ), BF16)], axis=2)
        lat_p = _attn_prompt(qt_p, kv_p, vt_p)
        q_s = q_s.reshape(DEC_BATCH, DEC_SEQ, N_HEADS, Q_HEAD)
        q_s = jnp.swapaxes(q_s, 1, 2).reshape(DEC_BATCH, N_HEADS * DEC_SEQ, Q_HEAD)
        kv_s = kvb[N_PROMPT:].reshape(DEC_BATCH, DEC_SEQ, Q_HEAD)
        kc_new = jnp.pad(kv_s[..., :KV_LORA], ((0, 0), (0, LANES - DEC_SEQ), (0, 0)))
        krt_new = jnp.pad(jnp.swapaxes(kv_s[..., KV_LORA:KV_LORA + QK_ROPE], 1, 2),
                          ((0, 0), (0, 0), (0, LANES - DEC_SEQ)))
        lat_s = _attn_sample(l, pt_flat, q_s, kc_new, krt_new, cache_ckv, cache_krope_t)
        lat_s = jnp.swapaxes(lat_s.reshape(DEC_BATCH, N_HEADS, DEC_SEQ, KV_LORA), 1, 2)
        lat = (lat_p.reshape(N_PROMPT, LAT_COLS), lat_s.reshape(N_SAMPLE, LAT_COLS))

        is_moe = l % 2 == 1
        wr = _pad_cols(w_router[l // 2], LANES) if is_moe else jnp.zeros((D_MODEL, LANES), F32)
        wrh, wrl = _split_hi_lo(wr)
        x, h, route = _merge(is_moe, o_pool, lat, o_conv, gates, x, _prep_w_uv(w_uv[l]),
                             w_br[l].astype(BF16), w_out[l].astype(BF16), g_ffn_norm[l][None],
                             wrh, wrl)
        if is_moe:
            pos, tile_expert, n_active = _route(route)
            xs = _dispatch(pos, h, jnp.zeros((M_SORTED,) + ROW_TILE, F32))
            ys = _expert_ffn(l // 2, tile_expert, n_active, xs, *expert_w)
            x = _combine(pos, ys, x, route, g_final[None], final_norm=(l == DEPTH - 1))
        else:
            x = _ffn(h, w_d_gate[l // 2].astype(BF16), w_d_up[l // 2].astype(BF16),
                     w_d_down[l // 2].astype(BF16), x)

        ckv_p.append(c_kv[:N_PROMPT].reshape(BATCH, SEQ, KV_LORA))
        kr_p.append(k_r[:N_PROMPT].reshape(BATCH, SEQ, QK_ROPE))
        pool_p.append(u_p[:, -POOL_BUF:])
        conv_p.append(z_p[:, -CONV_BUF:])
        ckv_s.append(c_kv[N_PROMPT:].reshape(DEC_BATCH, DEC_SEQ, KV_LORA))
        kr_s.append(k_r[N_PROMPT:].reshape(DEC_BATCH, DEC_SEQ, QK_ROPE))
        pool_s.append(jnp.concatenate([state_pool[l], u_s], axis=1)[:, -POOL_BUF:])
        conv_s.append(jnp.concatenate([state_conv[l], z_s], axis=1)[:, -CONV_BUF:])

    assert DEPTH % 2 == 0, "the last layer is an expert layer, whose combine applies the final norm"
    y = x
    return (y[:N_PROMPT].reshape(BATCH, SEQ, D_MODEL),
            y[N_PROMPT:].reshape(DEC_BATCH, DEC_SEQ, D_MODEL),
            jnp.stack(ckv_p), jnp.stack(kr_p), jnp.stack(pool_p), jnp.stack(conv_p),
            jnp.stack(ckv_s), jnp.stack(kr_s), jnp.stack(pool_s), jnp.stack(conv_s))
```
